```python
import jax, jax.numpy as jnp
from jax import lax
import numpy as np

D_MODEL = 1024
BATCH = 4
SEQ = 4096
DEPTH = 2

CHUNK = 64
N_MIXERS = 2
N_POOL_LAYERS = (DEPTH + N_MIXERS - 1) // N_MIXERS
N_GLA_LAYERS = DEPTH // N_MIXERS

POOL_WIDTH = D_MODEL
POOL_WINDOWS = (2, 4, 8, 16)
POOL_GROUPS = len(POOL_WINDOWS)
POOL_GROUP_DIM = POOL_WIDTH // POOL_GROUPS

GLA_HEADS = 4
GLA_KEY_WIDTH = D_MODEL // 2
GLA_VALUE_WIDTH = D_MODEL
GLA_HEAD_K = GLA_KEY_WIDTH // GLA_HEADS
GLA_HEAD_V = GLA_VALUE_WIDTH // GLA_HEADS
GLA_GATE_RANK = 16
GLA_GATE_NORMALIZER = 16.0
GLA_IN_WIDTH = 2 * GLA_KEY_WIDTH + 2 * GLA_VALUE_WIDTH + GLA_GATE_RANK

RMS_EPS = 1e-6

kernel_name = "hybrid_pool_gla_streaming_trunk"


def rms_norm(x, w):
    xf = x.astype(jnp.float32)
    y = xf * lax.rsqrt(jnp.mean(xf * xf, axis=-1, keepdims=True) + RMS_EPS)
    return (y * w.astype(jnp.float32)).astype(x.dtype)


def trailing_mean(u, window):
    seq = u.shape[1]
    cs = jnp.cumsum(u.astype(jnp.float32), axis=1)
    lagged = jnp.pad(cs, ((0, 0), (window, 0), (0, 0)))[:, :seq]
    count = jnp.minimum(jnp.arange(1, seq + 1), window).astype(jnp.float32)
    return ((cs - lagged) / count[None, :, None]).astype(u.dtype)


def pool_mixer(h, in_w, group_w, group_b, scale, out_w):
    b, s, _ = h.shape
    u, gate = jnp.split(h @ in_w, 2, axis=-1)
    ug = u.reshape(b, s, POOL_GROUPS, POOL_GROUP_DIM)
    pooled = jnp.stack([trailing_mean(ug[:, :, g], w) for g, w in enumerate(POOL_WINDOWS)],
                       axis=2) - ug
    mixed = jnp.einsum('bsgc,gcd->bsgd', pooled, group_w) + group_b
    y = mixed.reshape(b, s, POOL_WIDTH) * scale * jax.nn.silu(gate)
    return y @ out_w


def gla_mixer(h, in_w, gk_w, gk_b, head_norm_w, out_w):
    b, s, _ = h.shape
    n_chunks = s // CHUNK
    f32 = jnp.float32
    proj = h @ in_w
    q, k, v, gate, gk_low = jnp.split(
        proj, [GLA_KEY_WIDTH, 2 * GLA_KEY_WIDTH, 2 * GLA_KEY_WIDTH + GLA_VALUE_WIDTH,
               2 * GLA_KEY_WIDTH + 2 * GLA_VALUE_WIDTH], axis=-1)
    log_g = jax.nn.log_sigmoid((gk_low @ gk_w + gk_b).astype(f32)) / GLA_GATE_NORMALIZER

    def to_chunks(t, d):
        return t.astype(f32).reshape(b, n_chunks, CHUNK, GLA_HEADS, d)

    qc = to_chunks(q, GLA_HEAD_K) * (GLA_HEAD_K ** -0.5)
    kc = to_chunks(k, GLA_HEAD_K)
    vc = to_chunks(v, GLA_HEAD_V)
    cum = jnp.cumsum(to_chunks(log_g, GLA_HEAD_K), axis=2)
    cum_last = cum[:, :, -1:]
    e_pos, e_neg = jnp.exp(cum), jnp.exp(-cum)

    fwd = jnp.einsum('bnthk,bnshk->bnhts', qc * e_pos, kc * e_neg)
    bwd = jnp.einsum('bnthk,bnshk->bnhts', qc * e_neg, kc * e_pos)
    idx = jnp.arange(CHUNK)
    lower = idx[:, None] >= idx[None, :]
    scores = jnp.where(lower, fwd, bwd)
    o_intra = jnp.einsum('bnhts,bnshv->bnthv', scores, vc)

    q_dec = qc * e_pos
    k_dec = kc * jnp.exp(cum_last - cum)
    chunk_decay = jnp.exp(cum_last[:, :, 0])

    def step(state, inp):
        q_n, k_n, v_n, d_n = inp
        o_n = jnp.einsum('bthk,bhkv->bthv', q_n, state)
        state = state * d_n[..., None] + jnp.einsum('bthk,bthv->bhkv', k_n, v_n)
        return state, o_n

    xs = (jnp.moveaxis(q_dec, 1, 0), jnp.moveaxis(k_dec, 1, 0),
          jnp.moveaxis(vc, 1, 0), jnp.moveaxis(chunk_decay, 1, 0))
    state0 = jnp.zeros((b, GLA_HEADS, GLA_HEAD_K, GLA_HEAD_V), f32)
    _, o_inter = lax.scan(step, state0, xs)
    o = (o_intra + jnp.moveaxis(o_inter, 0, 1)).reshape(b, s, GLA_HEADS, GLA_HEAD_V)

    o = o * lax.rsqrt(jnp.mean(o * o, axis=-1, keepdims=True) + RMS_EPS) * head_norm_w.astype(f32)
    y = o.reshape(b, s, GLA_VALUE_WIDTH).astype(h.dtype) * jax.nn.silu(gate)
    return y @ out_w


def setup_inputs(seed: int = 0) -> dict:
    key = jax.random.key(seed)
    ks = jax.random.split(key, 16)
    nrm = jax.random.normal
    f32 = jnp.float32
    NP, NG = N_POOL_LAYERS, N_GLA_LAYERS
    return {
        "x": nrm(ks[0], (BATCH, SEQ, D_MODEL), f32),
        "norm_w": 1.0 + 0.02 * nrm(ks[1], (DEPTH, D_MODEL), f32),
        "pool_in_w": nrm(ks[2], (NP, D_MODEL, 2 * POOL_WIDTH), f32) * D_MODEL ** -0.5,
        "pool_group_w": nrm(ks[3], (NP, POOL_GROUPS, POOL_GROUP_DIM, POOL_GROUP_DIM), f32) * POOL_GROUP_DIM ** -0.5,
        "pool_group_b": 0.02 * nrm(ks[4], (NP, POOL_GROUPS, POOL_GROUP_DIM), f32),
        "pool_scale": 1.0 + 0.1 * nrm(ks[5], (NP, POOL_WIDTH), f32),
        "pool_out_w": nrm(ks[6], (NP, POOL_WIDTH, D_MODEL), f32) * POOL_WIDTH ** -0.5,
        "gla_in_w": nrm(ks[7], (NG, D_MODEL, GLA_IN_WIDTH), f32) * D_MODEL ** -0.5,
        "gla_gk_w": nrm(ks[8], (NG, GLA_GATE_RANK, GLA_KEY_WIDTH), f32) * GLA_GATE_RANK ** -0.5,
        "gla_gk_b": 0.02 * nrm(ks[9], (NG, GLA_KEY_WIDTH), f32),
        "gla_head_norm_w": 1.0 + 0.02 * nrm(ks[10], (NG, GLA_HEAD_V), f32),
        "gla_out_w": nrm(ks[11], (NG, GLA_VALUE_WIDTH, D_MODEL), f32) * GLA_VALUE_WIDTH ** -0.5,
        "final_norm_w": 1.0 + 0.02 * nrm(ks[12], (D_MODEL,), f32),
    }


def reference(x, norm_w, pool_in_w, pool_group_w, pool_group_b, pool_scale, pool_out_w,
              gla_in_w, gla_gk_w, gla_gk_b, gla_head_norm_w, gla_out_w, final_norm_w):
    h = x
    for i in range(DEPTH):
        normed = rms_norm(h, norm_w[i])
        j = i // N_MIXERS
        if i % N_MIXERS == 0:
            h = h + pool_mixer(normed, pool_in_w[j], pool_group_w[j], pool_group_b[j],
                               pool_scale[j], pool_out_w[j])
        else:
            h = h + gla_mixer(normed, gla_in_w[j], gla_gk_w[j], gla_gk_b[j],
                              gla_head_norm_w[j], gla_out_w[j])
    return rms_norm(h, final_norm_w)
```

```python
import functools

import jax
import jax.numpy as jnp
from jax import lax
from jax.experimental import pallas as pl
from jax.experimental.pallas import tpu as pltpu

F32 = jnp.float32
BF16 = jnp.bfloat16

RMS_EPS = 1e-6
CHUNK = 64
POOL_WINDOWS = (2, 4, 8, 16)
GLA_HEADS = 4
GLA_GATE_NORMALIZER = 16.0
GLA_GATE_RANK = 16

V7X_LANES = 128
V7X_SUBLANES = 8
V7X_VMEM_LIMIT_BYTES = 56 * 1024 * 1024

SEQ_TILE = 256
POOL_HALO = max(POOL_WINDOWS)
POOL_OFF = POOL_HALO + V7X_SUBLANES


def _dot(a, b):
    return jnp.dot(a, b, preferred_element_type=F32)


def _dot_nt(a, b):
    return lax.dot_general(a, b, (((1,), (1,)), ((), ())), preferred_element_type=F32)


def _rms_norm(x, w):
    return x * lax.rsqrt(jnp.mean(x * x, axis=-1, keepdims=True) + RMS_EPS) * w


def _silu(x):
    return x / (1.0 + jnp.exp(-x))


def _log_sigmoid(z):
    return jnp.minimum(z, 0.0) - jnp.log1p(jnp.exp(-jnp.abs(z)))


def _split_bf16(x):
    hi = x.astype(BF16)
    lo = (x - hi.astype(F32)).astype(BF16)
    return hi, lo


def _fused_kernel(x_ref, nw_ref, pin_ref, gw_ref, gb_ref, ps_ref, pout_ref,
                  gin_ref, wkt_ref, glw_ref, glwt_ref, gkw_ref, gkwt_ref, gkb_ref, gkbt_ref,
                  hn_ref, gout_ref, fnw_ref,
                  o_ref,
                  uext, s1, s2, s3, state,
                  *, ts, d_model, head_k, head_v):
    j = pl.program_id(1)
    n_groups = len(POOL_WINDOWS)
    gdim = d_model // n_groups
    n_chunks = ts // CHUNK
    key_w = GLA_HEADS * head_k
    ext = ts + POOL_HALO
    lo_row = V7X_SUBLANES

    @pl.when(jnp.logical_and(pl.program_id(0) == 0, j == 0))
    def _():
        uext[0:lo_row, :] = jnp.zeros((lo_row, d_model), F32)
        s1[0:lo_row, :] = jnp.zeros((lo_row, s1.shape[1]), F32)
        s2[0:lo_row, :] = jnp.zeros((lo_row, s2.shape[1]), F32)
        s3[0:lo_row, :] = jnp.zeros((lo_row, s3.shape[1]), F32)

    @pl.when(j == 0)
    def _():
        uext[lo_row:POOL_OFF, :] = jnp.zeros((POOL_HALO, d_model), F32)
        state[...] = jnp.zeros_like(state)

    x = x_ref[0]

    normed = _rms_norm(x, nw_ref[0:1, :]).astype(BF16)
    p1 = _dot(normed, pin_ref[...])
    u = p1[:, :d_model]
    gate = p1[:, d_model:]
    uext[POOL_OFF:POOL_OFF + ts, :] = u

    a1 = uext[lo_row:lo_row + ext, :] + uext[lo_row - 1:lo_row - 1 + ext, :]
    s1[lo_row:lo_row + ext, :] = a1[:, gdim:]
    a2 = s1[lo_row:lo_row + ext, :] + s1[lo_row - 2:lo_row - 2 + ext, :]
    s2[lo_row:lo_row + ext, :] = a2[:, gdim:]
    a3 = s2[lo_row:lo_row + ext, :] + s2[lo_row - 4:lo_row - 4 + ext, :]
    s3[lo_row:lo_row + ext, :] = a3[:, gdim:]
    a4 = s3[lo_row:lo_row + ext, :] + s3[lo_row - 8:lo_row - 8 + ext, :]
    sums = (a1[POOL_HALO:, :gdim], a2[POOL_HALO:, :gdim], a3[POOL_HALO:, :gdim], a4[POOL_HALO:, :])

    uext[lo_row:POOL_OFF, :] = uext[ts + lo_row:ts + POOL_OFF, :]

    frame = j * ts + lax.broadcasted_iota(jnp.int32, (ts, 1), 0) + 1
    ys = []
    for g, w in enumerate(POOL_WINDOWS):
        cols = slice(g * gdim, (g + 1) * gdim)
        inv_count = 1.0 / jnp.minimum(frame, w).astype(F32)
        pooled = sums[g] * inv_count - u[:, cols]
        mixed = _dot(pooled.astype(BF16), gw_ref[g]) + gb_ref[:, cols]
        ys.append(mixed * ps_ref[:, cols] * _silu(gate[:, cols]))
    y = jnp.concatenate(ys, axis=-1).astype(BF16)
    h = x + _dot(y, pout_ref[...])

    n2 = _rms_norm(h, nw_ref[1:2, :]).astype(BF16)
    p2 = _dot(n2, gin_ref[...])
    q = p2[:, :key_w] * (head_k ** -0.5)
    v = p2[:, key_w:key_w + d_model].astype(BF16)
    gate2 = p2[:, key_w + d_model:]
    kt = _dot_nt(wkt_ref[...], n2)
    glow = _dot(n2, glw_ref[...]).astype(BF16)
    glowt = _dot_nt(glwt_ref[...], n2).astype(BF16)

    inv_norm = 1.0 / GLA_GATE_NORMALIZER
    lg = _log_sigmoid(_dot(glow, gkw_ref[...]) + gkb_ref[...]) * inv_norm
    lgt = _log_sigmoid(_dot(gkwt_ref[...], glowt) + gkbt_ref[...]) * inv_norm

    r = lax.broadcasted_iota(jnp.int32, (ts, ts), 0)
    c = lax.broadcasted_iota(jnp.int32, (ts, ts), 1)
    same_chunk = (r // CHUNK) == (c // CHUNK)
    causal = c <= r
    tril = jnp.where(same_chunk & causal, 1.0, 0.0).astype(BF16)
    triu = jnp.where(same_chunk & (r <= c), 1.0, 0.0).astype(BF16)
    cr = lax.broadcasted_iota(jnp.int32, (ts, n_chunks * V7X_LANES), 0)
    cc = lax.broadcasted_iota(jnp.int32, (ts, n_chunks * V7X_LANES), 1)
    chunk_ones = jnp.where((cr // CHUNK) == (cc // V7X_LANES), 1.0, 0.0).astype(BF16)
    block_ones = jnp.where(same_chunk, 1.0, 0.0).astype(BF16)

    lg_hi, lg_lo = _split_bf16(lg)
    lgt_hi, lgt_lo = _split_bf16(lgt)
    cum = _dot(tril, lg_hi) + _dot(tril, lg_lo)
    cumt = _dot(lgt_hi, triu) + _dot(lgt_lo, triu)
    tott = _dot(lgt_hi, block_ones) + _dot(lgt_lo, block_ones)
    decay = jnp.exp(_dot(lgt_hi, chunk_ones) + _dot(lgt_lo, chunk_ones))

    q_pos = (q * jnp.exp(cum)).astype(BF16)
    q_neg = (q * jnp.exp(-cum)).astype(BF16)
    k_neg = (kt * jnp.exp(-cumt)).astype(BF16)
    k_pos = (kt * jnp.exp(cumt)).astype(BF16)
    k_dec = kt * jnp.exp(tott - cumt)
    lane_chunk = lax.broadcasted_iota(jnp.int32, (key_w, ts), 1) // CHUNK
    k_dec_even = jnp.where(lane_chunk % 2 == 0, k_dec, 0.0).astype(BF16)
    k_dec_odd = jnp.where(lane_chunk % 2 == 1, k_dec, 0.0).astype(BF16)

    outs = []
    for hd in range(GLA_HEADS):
        ks = slice(hd * head_k, (hd + 1) * head_k)
        vs = slice(hd * head_v, (hd + 1) * head_v)
        v_h = v[:, vs]
        fwd = _dot(q_pos[:, ks], k_neg[ks, :])
        bwd = _dot(q_neg[:, ks], k_pos[ks, :])
        scores = jnp.where(same_chunk, jnp.where(causal, fwd, bwd), 0.0).astype(BF16)
        o_h = _dot(scores, v_h)

        st = state[hd]
        inter = []
        for n in range(n_chunks):
            rows = slice(n * CHUNK, (n + 1) * CHUNK)
            inter.append(_dot(q_pos[rows, ks], st.astype(BF16)))
            pair = (n // 2) * 2 * CHUNK
            k_n = (k_dec_even if n % 2 == 0 else k_dec_odd)[ks, pair:pair + 2 * CHUNK]
            d_n = decay[ks, n * V7X_LANES:(n + 1) * V7X_LANES]
            d_n = jnp.concatenate([d_n] * (head_v // V7X_LANES), axis=-1)
            st = st * d_n + _dot(k_n, v_h[pair:pair + 2 * CHUNK, :])
        state[hd] = st
        o_h = o_h + jnp.concatenate(inter, axis=0)
        o_h = o_h * lax.rsqrt(jnp.mean(o_h * o_h, axis=-1, keepdims=True) + RMS_EPS) * hn_ref[:, vs]
        outs.append(o_h)
    y2 = (jnp.concatenate(outs, axis=-1) * _silu(gate2)).astype(BF16)
    h2 = h + _dot(y2, gout_ref[...])

    o_ref[0] = _rms_norm(h2, fnw_ref[...])


def _resident(shape):
    zeros = (0,) * len(shape)
    return pl.BlockSpec(shape, lambda b, j: zeros, pipeline_mode=pl.Buffered(1))


def kernel(x, norm_w, pool_in_w, pool_group_w, pool_group_b, pool_scale, pool_out_w,
           gla_in_w, gla_gk_w, gla_gk_b, gla_head_norm_w, gla_out_w, final_norm_w):
    batch, seq, d_model = x.shape
    ts = SEQ_TILE
    assert seq % ts == 0 and ts % (2 * CHUNK) == 0
    assert norm_w.shape[0] == 2 and pool_in_w.shape[0] == 1 and gla_in_w.shape[0] == 1
    key_w = gla_gk_w.shape[-1]
    head_k = key_w // GLA_HEADS
    head_v = d_model // GLA_HEADS
    n_groups = len(POOL_WINDOWS)
    gdim = d_model // n_groups
    assert gla_in_w.shape[-1] == 2 * key_w + 2 * d_model + GLA_GATE_RANK

    w_in = gla_in_w[0]
    w_q, w_k = w_in[:, :key_w], w_in[:, key_w:2 * key_w]
    w_vg = w_in[:, 2 * key_w:2 * key_w + 2 * d_model]
    w_low = w_in[:, 2 * key_w + 2 * d_model:]
    gin = jnp.concatenate([w_q, w_vg], axis=1).astype(BF16)
    wkt = w_k.T.astype(BF16)
    glw = jnp.pad(w_low, ((0, 0), (0, V7X_LANES - GLA_GATE_RANK))).astype(BF16)
    glwt = w_low.T.astype(BF16)
    gkw = jnp.pad(gla_gk_w[0], ((0, V7X_LANES - GLA_GATE_RANK), (0, 0))).astype(BF16)
    gkwt = gla_gk_w[0].T.astype(BF16)
    gkb = gla_gk_b[0].reshape(1, key_w)
    gkbt = gla_gk_b[0].reshape(key_w, 1)
    hn = jnp.tile(gla_head_norm_w[0], GLA_HEADS).reshape(1, d_model)

    operands = (
        x, norm_w, pool_in_w[0].astype(BF16), pool_group_w[0].astype(BF16),
        pool_group_b[0].reshape(1, d_model), pool_scale[0].reshape(1, d_model),
        pool_out_w[0].astype(BF16),
        gin, wkt, glw, glwt, gkw, gkwt, gkb, gkbt, hn, gla_out_w[0].astype(BF16),
        final_norm_w.reshape(1, d_model),
    )
    tile = pl.BlockSpec((1, ts, d_model), lambda b, j: (b, j, 0))
    in_specs = [tile] + [_resident(op.shape) for op in operands[1:]]
    rows = ts + POOL_OFF
    body = functools.partial(_fused_kernel, ts=ts, d_model=d_model, head_k=head_k, head_v=head_v)
    return pl.pallas_call(
        body,
        grid=(batch, seq // ts),
        in_specs=in_specs,
        out_specs=tile,
        out_shape=jax.ShapeDtypeStruct(x.shape, x.dtype),
        scratch_shapes=[
            pltpu.VMEM((rows, d_model), F32),
            pltpu.VMEM((rows, d_model - gdim), F32),
            pltpu.VMEM((rows, d_model - 2 * gdim), F32),
            pltpu.VMEM((rows, d_model - 3 * gdim), F32),
            pltpu.VMEM((GLA_HEADS, head_k, head_v), F32),
        ],
        compiler_params=pltpu.CompilerParams(
            dimension_semantics=("arbitrary", "arbitrary"),
            vmem_limit_bytes=V7X_VMEM_LIMIT_BYTES,
        ),
        name="pool_gla_trunk",
    )(*operands)
```

```python
import functools

import jax
import jax.numpy as jnp
from jax import lax
from jax.experimental import pallas as pl
from jax.experimental.pallas import tpu as pltpu

F32 = jnp.float32
BF16 = jnp.bfloat16

RMS_EPS = 1e-6
CHUNK = 64
POOL_WINDOWS = (2, 4, 8, 16)
GLA_HEADS = 4
GLA_GATE_NORMALIZER = 16.0
GLA_GATE_RANK = 16

V7X_LANES = 128
V7X_SUBLANES = 8
V7X_VMEM_LIMIT_BYTES = 56 * 1024 * 1024

SEQ_TILE = 512
SUB_ROWS = 256
POOL_HALO = max(POOL_WINDOWS)
POOL_OFF = POOL_HALO + V7X_SUBLANES


def _dot(a, b):
    return jnp.dot(a, b, preferred_element_type=F32)


def _dot_nt(a, b):
    return lax.dot_general(a, b, (((1,), (1,)), ((), ())), preferred_element_type=F32)


def _rms_norm(x, w):
    return x * lax.rsqrt(jnp.mean(x * x, axis=-1, keepdims=True) + RMS_EPS) * w


def _silu(x):
    return x / (1.0 + jnp.exp(-x))


def _log_sigmoid(z):
    return jnp.minimum(z, 0.0) - jnp.log1p(jnp.exp(-jnp.abs(z)))


def _split_bf16(x):
    hi = x.astype(BF16)
    lo = (x - hi.astype(F32)).astype(BF16)
    return hi, lo


def _pool_layer(x, carry, first_frame, w, uext, s1, s2, s3):
    nw_ref, pin_ref, gw_ref, gb_ref, ps_ref, pout_ref = w
    rows, d_model = x.shape
    gdim = d_model // len(POOL_WINDOWS)
    ext = rows + POOL_HALO
    lo_row = V7X_SUBLANES

    normed = _rms_norm(x, nw_ref[0:1, :]).astype(BF16)
    u = _dot(normed, pin_ref[:, :d_model])
    uext[lo_row:POOL_OFF, :] = carry["u_tail"]
    uext[POOL_OFF:POOL_OFF + rows, :] = u
    carry["u_tail"] = u[rows - POOL_HALO:, :]
    yield
    gate = _dot(normed, pin_ref[:, d_model:])

    a1 = uext[lo_row:lo_row + ext, :] + uext[lo_row - 1:lo_row - 1 + ext, :]
    s1[lo_row:lo_row + ext, :] = a1[:, gdim:]
    a2 = s1[lo_row:lo_row + ext, :] + s1[lo_row - 2:lo_row - 2 + ext, :]
    s2[lo_row:lo_row + ext, :] = a2[:, gdim:]
    a3 = s2[lo_row:lo_row + ext, :] + s2[lo_row - 4:lo_row - 4 + ext, :]
    s3[lo_row:lo_row + ext, :] = a3[:, gdim:]
    a4 = s3[lo_row:lo_row + ext, :] + s3[lo_row - 8:lo_row - 8 + ext, :]
    sums = (a1[POOL_HALO:, :gdim], a2[POOL_HALO:, :gdim], a3[POOL_HALO:, :gdim], a4[POOL_HALO:, :])

    frame = first_frame + lax.broadcasted_iota(jnp.int32, (rows, 1), 0) + 1
    ys = []
    for g, win in enumerate(POOL_WINDOWS):
        cols = slice(g * gdim, (g + 1) * gdim)
        inv_count = 1.0 / jnp.minimum(frame, win).astype(F32)
        pooled = sums[g] * inv_count - u[:, cols]
        mixed = _dot(pooled.astype(BF16), gw_ref[g]) + gb_ref[:, cols]
        ys.append(mixed * ps_ref[:, cols] * _silu(gate[:, cols]))
    y = jnp.concatenate(ys, axis=-1).astype(BF16)
    yield
    return x + _dot(y, pout_ref[...])


def _gla_layer(h, carry, w, head_k, head_v):
    (nw_ref, gin_ref, wkt_ref, glw_ref, glwt_ref, gkw_ref, gkwt_ref, gkb_ref, gkbt_ref,
     hn_ref, gout_ref) = w
    rows, d_model = h.shape
    n_chunks = rows // CHUNK
    key_w = GLA_HEADS * head_k

    n2 = _rms_norm(h, nw_ref[1:2, :]).astype(BF16)
    glow = _dot(n2, glw_ref[...]).astype(BF16)
    glowt = _dot_nt(glwt_ref[...], n2).astype(BF16)
    kt = _dot_nt(wkt_ref[...], n2)
    z = _dot(glow, gkw_ref[...]) + gkb_ref[...]
    zt = _dot(gkwt_ref[...], glowt) + gkbt_ref[...]
    yield

    v = _dot(n2, gin_ref[:, key_w:key_w + d_model]).astype(BF16)
    inv_norm = 1.0 / GLA_GATE_NORMALIZER
    lg = _log_sigmoid(z) * inv_norm
    lgt = _log_sigmoid(zt) * inv_norm

    r = lax.broadcasted_iota(jnp.int32, (rows, rows), 0)
    c = lax.broadcasted_iota(jnp.int32, (rows, rows), 1)
    same_chunk = (r // CHUNK) == (c // CHUNK)
    causal = c <= r
    tril = jnp.where(same_chunk & causal, 1.0, 0.0).astype(BF16)
    triu = jnp.where(same_chunk & (r <= c), 1.0, 0.0).astype(BF16)
    cr = lax.broadcasted_iota(jnp.int32, (rows, n_chunks * V7X_LANES), 0)
    cc = lax.broadcasted_iota(jnp.int32, (rows, n_chunks * V7X_LANES), 1)
    chunk_ones = jnp.where((cr // CHUNK) == (cc // V7X_LANES), 1.0, 0.0).astype(BF16)
    block_ones = jnp.where(same_chunk, 1.0, 0.0).astype(BF16)

    lg_hi, lg_lo = _split_bf16(lg)
    lgt_hi, lgt_lo = _split_bf16(lgt)
    yield
    cum = _dot(tril, lg_hi) + _dot(tril, lg_lo)
    cumt = _dot(lgt_hi, triu) + _dot(lgt_lo, triu)
    tott = _dot(lgt_hi, block_ones) + _dot(lgt_lo, block_ones)
    decay = jnp.exp(_dot(lgt_hi, chunk_ones) + _dot(lgt_lo, chunk_ones))
    q = _dot(n2, gin_ref[:, :key_w]) * (head_k ** -0.5)
    gate2 = _dot(n2, gin_ref[:, key_w + d_model:])

    q_pos = (q * jnp.exp(cum)).astype(BF16)
    q_neg = (q * jnp.exp(-cum)).astype(BF16)
    k_neg = (kt * jnp.exp(-cumt)).astype(BF16)
    k_pos = (kt * jnp.exp(cumt)).astype(BF16)
    k_dec = kt * jnp.exp(tott - cumt)
    lane_chunk = lax.broadcasted_iota(jnp.int32, (key_w, rows), 1) // CHUNK
    k_dec_even = jnp.where(lane_chunk % 2 == 0, k_dec, 0.0).astype(BF16)
    k_dec_odd = jnp.where(lane_chunk % 2 == 1, k_dec, 0.0).astype(BF16)

    heads = range(GLA_HEADS)
    ksl = [slice(hd * head_k, (hd + 1) * head_k) for hd in heads]
    vsl = [slice(hd * head_v, (hd + 1) * head_v) for hd in heads]
    v_hs = [v[:, vs] for vs in vsl]
    yield

    fwd = [_dot(q_pos[:, ks], k_neg[ks, :]) for ks in ksl]
    bwd = [_dot(q_neg[:, ks], k_pos[ks, :]) for ks in ksl]
    upd = []
    for hd in heads:
        upd_h = []
        for n in range(n_chunks):
            pair = (n // 2) * 2 * CHUNK
            k_n = (k_dec_even if n % 2 == 0 else k_dec_odd)[ksl[hd], pair:pair + 2 * CHUNK]
            upd_h.append(_dot(k_n, v_hs[hd][pair:pair + 2 * CHUNK, :]))
        upd.append(upd_h)
    yield

    scores = [jnp.where(same_chunk, jnp.where(causal, fwd[hd], bwd[hd]), 0.0).astype(BF16)
              for hd in heads]
    entering = []
    for hd in heads:
        st = carry["states"][hd]
        ent_h = []
        for n in range(n_chunks):
            ent_h.append(st.astype(BF16))
            d_n = decay[ksl[hd], n * V7X_LANES:(n + 1) * V7X_LANES]
            d_n = jnp.concatenate([d_n] * (head_v // V7X_LANES), axis=-1)
            st = st * d_n + upd[hd][n]
        carry["states"][hd] = st
        entering.append(ent_h)
    yield

    outs = []
    for hd in heads:
        inter = [_dot(q_pos[n * CHUNK:(n + 1) * CHUNK, ksl[hd]], entering[hd][n])
                 for n in range(n_chunks)]
        o_h = _dot(scores[hd], v_hs[hd]) + jnp.concatenate(inter, axis=0)
        o_h = o_h * lax.rsqrt(jnp.mean(o_h * o_h, axis=-1, keepdims=True) + RMS_EPS) * hn_ref[:, vsl[hd]]
        outs.append(o_h)
    yield
    y2 = (jnp.concatenate(outs, axis=-1) * _silu(gate2)).astype(BF16)
    return h + _dot(y2, gout_ref[...])


def _sub_block(x_ref, o_ref, rows, first_frame, carry, pool_w, gla_w, fnw_ref, pool_scratch,
               head_k, head_v):
    h = yield from _pool_layer(x_ref[0, rows, :], carry, first_frame, pool_w, *pool_scratch)
    yield
    h2 = yield from _gla_layer(h, carry, gla_w, head_k, head_v)
    o_ref[0, rows, :] = _rms_norm(h2, fnw_ref[...])


def _fused_kernel(x_ref, nw_ref, pin_ref, gw_ref, gb_ref, ps_ref, pout_ref,
                  gin_ref, wkt_ref, glw_ref, glwt_ref, gkw_ref, gkwt_ref, gkb_ref, gkbt_ref,
                  hn_ref, gout_ref, fnw_ref,
                  o_ref,
                  uext, s1, s2, s3, halo, state,
                  *, ts, d_model, head_k, head_v):
    j = pl.program_id(1)
    n_sub = ts // SUB_ROWS
    lo_row = V7X_SUBLANES

    @pl.when(jnp.logical_and(pl.program_id(0) == 0, j == 0))
    def _():
        for buf in (uext, s1, s2, s3):
            buf[:, 0:lo_row, :] = jnp.zeros((n_sub, lo_row, buf.shape[2]), F32)

    @pl.when(j == 0)
    def _():
        halo[...] = jnp.zeros_like(halo)
        state[...] = jnp.zeros_like(state)

    pool_w = (nw_ref, pin_ref, gw_ref, gb_ref, ps_ref, pout_ref)
    gla_w = (nw_ref, gin_ref, wkt_ref, glw_ref, glwt_ref, gkw_ref, gkwt_ref, gkb_ref, gkbt_ref,
             hn_ref, gout_ref)

    carry = {"u_tail": halo[...], "states": [state[hd] for hd in range(GLA_HEADS)]}
    blocks = [
        _sub_block(x_ref, o_ref, slice(s * SUB_ROWS, (s + 1) * SUB_ROWS), j * ts + s * SUB_ROWS,
                   carry, pool_w, gla_w, fnw_ref, (uext.at[s], s1.at[s], s2.at[s], s3.at[s]),
                   head_k, head_v)
        for s in range(n_sub)
    ]
    while blocks:
        for blk in list(blocks):
            try:
                next(blk)
            except StopIteration:
                blocks.remove(blk)
    halo[...] = carry["u_tail"]
    for hd in range(GLA_HEADS):
        state[hd] = carry["states"][hd]


def _resident(shape):
    zeros = (0,) * len(shape)
    return pl.BlockSpec(shape, lambda b, j: zeros, pipeline_mode=pl.Buffered(1))


def kernel(x, norm_w, pool_in_w, pool_group_w, pool_group_b, pool_scale, pool_out_w,
           gla_in_w, gla_gk_w, gla_gk_b, gla_head_norm_w, gla_out_w, final_norm_w):
    batch, seq, d_model = x.shape
    ts = SEQ_TILE
    assert seq % ts == 0 and ts % SUB_ROWS == 0 and SUB_ROWS % (2 * CHUNK) == 0
    assert norm_w.shape[0] == 2 and pool_in_w.shape[0] == 1 and gla_in_w.shape[0] == 1
    key_w = gla_gk_w.shape[-1]
    head_k = key_w // GLA_HEADS
    head_v = d_model // GLA_HEADS
    n_groups = len(POOL_WINDOWS)
    gdim = d_model // n_groups
    assert gla_in_w.shape[-1] == 2 * key_w + 2 * d_model + GLA_GATE_RANK

    w_in = gla_in_w[0]
    w_q, w_k = w_in[:, :key_w], w_in[:, key_w:2 * key_w]
    w_vg = w_in[:, 2 * key_w:2 * key_w + 2 * d_model]
    w_low = w_in[:, 2 * key_w + 2 * d_model:]
    gin = jnp.concatenate([w_q, w_vg], axis=1).astype(BF16)
    wkt = w_k.T.astype(BF16)
    glw = jnp.pad(w_low, ((0, 0), (0, V7X_LANES - GLA_GATE_RANK))).astype(BF16)
    glwt = w_low.T.astype(BF16)
    gkw = jnp.pad(gla_gk_w[0], ((0, V7X_LANES - GLA_GATE_RANK), (0, 0))).astype(BF16)
    gkwt = gla_gk_w[0].T.astype(BF16)
    gkb = gla_gk_b[0].reshape(1, key_w)
    gkbt = gla_gk_b[0].reshape(key_w, 1)
    hn = jnp.tile(gla_head_norm_w[0], GLA_HEADS).reshape(1, d_model)

    operands = (
        x, norm_w, pool_in_w[0].astype(BF16), pool_group_w[0].astype(BF16),
        pool_group_b[0].reshape(1, d_model), pool_scale[0].reshape(1, d_model),
        pool_out_w[0].astype(BF16),
        gin, wkt, glw, glwt, gkw, gkwt, gkb, gkbt, hn, gla_out_w[0].astype(BF16),
        final_norm_w.reshape(1, d_model),
    )
    tile = pl.BlockSpec((1, ts, d_model), lambda b, j: (b, j, 0))
    in_specs = [tile] + [_resident(op.shape) for op in operands[1:]]
    n_sub = ts // SUB_ROWS
    rows = SUB_ROWS + POOL_OFF
    body = functools.partial(_fused_kernel, ts=ts, d_model=d_model, head_k=head_k, head_v=head_v)
    return pl.pallas_call(
        body,
        grid=(batch, seq // ts),
        in_specs=in_specs,
        out_specs=tile,
        out_shape=jax.ShapeDtypeStruct(x.shape, x.dtype),
        scratch_shapes=[
            pltpu.VMEM((n_sub, rows, d_model), F32),
            pltpu.VMEM((n_sub, rows, d_model - gdim), F32),
            pltpu.VMEM((n_sub, rows, d_model - 2 * gdim), F32),
            pltpu.VMEM((n_sub, rows, d_model - 3 * gdim), F32),
            pltpu.VMEM((POOL_HALO, d_model), F32),
            pltpu.VMEM((GLA_HEADS, head_k, head_v), F32),
        ],
        compiler_params=pltpu.CompilerParams(
            dimension_semantics=("arbitrary", "arbitrary"),
            vmem_limit_bytes=V7X_VMEM_LIMIT_BYTES,
        ),
        name="pool_gla_trunk",
    )(*operands)
```

```python
import functools

import jax
import jax.numpy as jnp
from jax import lax
from jax.experimental import pallas as pl
from jax.experimental.pallas import tpu as pltpu

F32 = jnp.float32
BF16 = jnp.bfloat16

RMS_EPS = 1e-6
CHUNK = 64
POOL_WINDOWS = (2, 4, 8, 16)
GLA_HEADS = 4
GLA_GATE_NORMALIZER = 16.0
GLA_GATE_RANK = 16

V7X_LANES = 128
V7X_SUBLANES = 8
V7X_VMEM_LIMIT_BYTES = 56 * 1024 * 1024

SEQ_TILE = 512
SUB_ROWS = 256
POOL_HALO = max(POOL_WINDOWS)
POOL_OFF = POOL_HALO + V7X_SUBLANES


def _dot(a, b):
    return jnp.dot(a, b, preferred_element_type=F32)


def _dot_nt(a, b):
    return lax.dot_general(a, b, (((1,), (1,)), ((), ())), preferred_element_type=F32)


def _dot_tn(a, b):
    return lax.dot_general(a, b, (((0,), (0,)), ((), ())), preferred_element_type=F32)


def _rms_norm(x, w):
    return x * lax.rsqrt(jnp.mean(x * x, axis=-1, keepdims=True) + RMS_EPS) * w


def _silu(x):
    return x / (1.0 + jnp.exp(-x))


def _log_sigmoid(z):
    return jnp.minimum(z, 0.0) - jnp.log1p(jnp.exp(-jnp.abs(z)))


def _split_bf16(x):
    hi = x.astype(BF16)
    lo = (x - hi.astype(F32)).astype(BF16)
    return hi, lo


def _pool_layer(x, carry, first_frame, w, uext, s1, s2, s3):
    nw_ref, pin_ref, gw_ref, gb_ref, ps_ref, pout_ref = w
    rows, d_model = x.shape
    gdim = d_model // len(POOL_WINDOWS)
    ext = rows + POOL_HALO
    lo_row = V7X_SUBLANES

    normed = _rms_norm(x, nw_ref[0:1, :]).astype(BF16)
    u = _dot(normed, pin_ref[:, :d_model])
    uext[lo_row:POOL_OFF, :] = carry["u_tail"]
    uext[POOL_OFF:POOL_OFF + rows, :] = u
    carry["u_tail"] = u[rows - POOL_HALO:, :]
    yield
    gate = _dot(normed, pin_ref[:, d_model:])

    a1 = uext[lo_row:lo_row + ext, :] + uext[lo_row - 1:lo_row - 1 + ext, :]
    s1[lo_row:lo_row + ext, :] = a1[:, gdim:]
    a2 = s1[lo_row:lo_row + ext, :] + s1[lo_row - 2:lo_row - 2 + ext, :]
    s2[lo_row:lo_row + ext, :] = a2[:, gdim:]
    a3 = s2[lo_row:lo_row + ext, :] + s2[lo_row - 4:lo_row - 4 + ext, :]
    s3[lo_row:lo_row + ext, :] = a3[:, gdim:]
    a4 = s3[lo_row:lo_row + ext, :] + s3[lo_row - 8:lo_row - 8 + ext, :]
    sums = (a1[POOL_HALO:, :gdim], a2[POOL_HALO:, :gdim], a3[POOL_HALO:, :gdim], a4[POOL_HALO:, :])

    frame = first_frame + lax.broadcasted_iota(jnp.int32, (rows, 1), 0) + 1
    ys = []
    for g, win in enumerate(POOL_WINDOWS):
        cols = slice(g * gdim, (g + 1) * gdim)
        inv_count = 1.0 / jnp.minimum(frame, win).astype(F32)
        pooled = sums[g] * inv_count - u[:, cols]
        mixed = _dot(pooled.astype(BF16), gw_ref[g]) + gb_ref[:, cols]
        ys.append(mixed * ps_ref[:, cols] * _silu(gate[:, cols]))
    y = jnp.concatenate(ys, axis=-1).astype(BF16)
    yield
    return x + _dot(y, pout_ref[...])


def _gla_layer(h, carry, w, head_k, head_v):
    nw_ref, gin_ref, glwt_ref, gkw_ref, gkb_ref, hn_ref, gout_ref = w
    rows, d_model = h.shape
    n_chunks = rows // CHUNK
    key_w = GLA_HEADS * head_k
    heads = range(GLA_HEADS)
    ksl = [slice(hd * head_k, (hd + 1) * head_k) for hd in heads]
    vsl = [slice(hd * head_v, (hd + 1) * head_v) for hd in heads]
    csl = [slice(n * CHUNK, (n + 1) * CHUNK) for n in range(n_chunks)]

    n2 = _rms_norm(h, nw_ref[1:2, :]).astype(BF16)
    glow = _dot_nt(glwt_ref[...], n2).T.astype(BF16)
    qk = _dot(n2, gin_ref[:, :2 * key_w])
    z = _dot(glow, gkw_ref[...]) + gkb_ref[...]
    yield

    vg = _dot(n2, gin_ref[:, 2 * key_w:])
    lg = _log_sigmoid(z) * (1.0 / GLA_GATE_NORMALIZER)
    lg_hi, lg_lo = _split_bf16(lg)
    r = lax.broadcasted_iota(jnp.int32, (rows, rows), 0)
    c = lax.broadcasted_iota(jnp.int32, (rows, rows), 1)
    same_chunk = (r // CHUNK) == (c // CHUNK)
    causal = c <= r
    tril = jnp.where(same_chunk & causal, 1.0, 0.0).astype(BF16)
    cum = _dot(tril, lg_hi) + _dot(tril, lg_lo)
    yield

    totals = [cum[cs.stop - 1:cs.stop, :] for cs in csl]
    tot = jnp.concatenate([jnp.broadcast_to(t, (CHUNK, key_w)) for t in totals], axis=0)
    e_pos = jnp.exp(cum)
    e_neg = jnp.exp(-cum)
    q = qk[:, :key_w] * (head_k ** -0.5)
    k = qk[:, key_w:]
    q_pos = (q * e_pos).astype(BF16)
    q_neg = (q * e_neg).astype(BF16)
    k_neg = (k * e_neg).astype(BF16)
    k_pos = (k * e_pos).astype(BF16)
    k_dec = (k * jnp.exp(tot - cum)).astype(BF16)
    v = vg[:, :d_model].astype(BF16)
    gate2 = vg[:, d_model:]
    pad = jnp.zeros((V7X_LANES - n_chunks, key_w), F32)
    chunk_decay = jnp.exp(jnp.concatenate(totals + [pad], axis=0))
    yield

    fwd = [_dot_nt(q_pos[:, ks], k_neg[:, ks]) for ks in ksl]
    bwd = [_dot_nt(q_neg[:, ks], k_pos[:, ks]) for ks in ksl]
    upd = [[_dot_tn(k_dec[cs, ks], v[cs, vs]) for cs in csl]
           for ks, vs in zip(ksl, vsl)]
    decay_t = [chunk_decay[:, ks].T for ks in ksl]
    yield

    scores = [jnp.where(same_chunk, jnp.where(causal, fwd[hd], bwd[hd]), 0.0).astype(BF16)
              for hd in heads]
    entering = []
    for hd in heads:
        st = carry["states"][hd]
        ent_h = []
        for n in range(n_chunks):
            ent_h.append(st.astype(BF16))
            st = st * decay_t[hd][:, n:n + 1] + upd[hd][n]
        carry["states"][hd] = st
        entering.append(ent_h)
    yield

    outs = []
    for hd in heads:
        inter = [_dot(q_pos[cs, ksl[hd]], entering[hd][n]) for n, cs in enumerate(csl)]
        o_h = _dot(scores[hd], v[:, vsl[hd]]) + jnp.concatenate(inter, axis=0)
        o_h = o_h * lax.rsqrt(jnp.mean(o_h * o_h, axis=-1, keepdims=True) + RMS_EPS) * hn_ref[:, vsl[hd]]
        outs.append(o_h)
    yield
    y2 = (jnp.concatenate(outs, axis=-1) * _silu(gate2)).astype(BF16)
    return h + _dot(y2, gout_ref[...])


def _sub_block(x_ref, o_ref, rows, first_frame, carry, pool_w, gla_w, fnw_ref, pool_scratch,
               head_k, head_v):
    h = yield from _pool_layer(x_ref[0, rows, :], carry, first_frame, pool_w, *pool_scratch)
    yield
    h2 = yield from _gla_layer(h, carry, gla_w, head_k, head_v)
    o_ref[0, rows, :] = _rms_norm(h2, fnw_ref[...])


def _fused_kernel(x_ref, nw_ref, pin_ref, gw_ref, gb_ref, ps_ref, pout_ref,
                  gin_ref, glwt_ref, gkw_ref, gkb_ref, hn_ref, gout_ref, fnw_ref,
                  o_ref,
                  uext, s1, s2, s3, halo, state,
                  *, ts, d_model, head_k, head_v):
    j = pl.program_id(1)
    n_sub = ts // SUB_ROWS
    lo_row = V7X_SUBLANES

    @pl.when(jnp.logical_and(pl.program_id(0) == 0, j == 0))
    def _():
        for buf in (uext, s1, s2, s3):
            buf[:, 0:lo_row, :] = jnp.zeros((n_sub, lo_row, buf.shape[2]), F32)

    @pl.when(j == 0)
    def _():
        halo[...] = jnp.zeros_like(halo)
        state[...] = jnp.zeros_like(state)

    pool_w = (nw_ref, pin_ref, gw_ref, gb_ref, ps_ref, pout_ref)
    gla_w = (nw_ref, gin_ref, glwt_ref, gkw_ref, gkb_ref, hn_ref, gout_ref)

    carry = {"u_tail": halo[...], "states": [state[hd] for hd in range(GLA_HEADS)]}
    blocks = [
        _sub_block(x_ref, o_ref, slice(s * SUB_ROWS, (s + 1) * SUB_ROWS), j * ts + s * SUB_ROWS,
                   carry, pool_w, gla_w, fnw_ref, (uext.at[s], s1.at[s], s2.at[s], s3.at[s]),
                   head_k, head_v)
        for s in range(n_sub)
    ]
    while blocks:
        for blk in list(blocks):
            try:
                next(blk)
            except StopIteration:
                blocks.remove(blk)
    halo[...] = carry["u_tail"]
    for hd in range(GLA_HEADS):
        state[hd] = carry["states"][hd]


def _resident(shape):
    zeros = (0,) * len(shape)
    return pl.BlockSpec(shape, lambda b, j: zeros, pipeline_mode=pl.Buffered(1))


def kernel(x, norm_w, pool_in_w, pool_group_w, pool_group_b, pool_scale, pool_out_w,
           gla_in_w, gla_gk_w, gla_gk_b, gla_head_norm_w, gla_out_w, final_norm_w):
    batch, seq, d_model = x.shape
    ts = SEQ_TILE
    assert seq % ts == 0 and ts % SUB_ROWS == 0 and SUB_ROWS % CHUNK == 0
    assert norm_w.shape[0] == 2 and pool_in_w.shape[0] == 1 and gla_in_w.shape[0] == 1
    key_w = gla_gk_w.shape[-1]
    head_k = key_w // GLA_HEADS
    head_v = d_model // GLA_HEADS
    n_groups = len(POOL_WINDOWS)
    gdim = d_model // n_groups
    qkvg_w = 2 * key_w + 2 * d_model
    assert gla_in_w.shape[-1] == qkvg_w + GLA_GATE_RANK

    rank_pad = V7X_LANES - GLA_GATE_RANK
    gin = gla_in_w[0, :, :qkvg_w].astype(BF16)
    glwt = jnp.pad(gla_in_w[0, :, qkvg_w:].T, ((0, rank_pad), (0, 0))).astype(BF16)
    gkw = jnp.pad(gla_gk_w[0], ((0, rank_pad), (0, 0))).astype(BF16)
    gkb = gla_gk_b[0].reshape(1, key_w)
    hn = jnp.tile(gla_head_norm_w[0], GLA_HEADS).reshape(1, d_model)

    operands = (
        x, norm_w, pool_in_w[0].astype(BF16), pool_group_w[0].astype(BF16),
        pool_group_b[0].reshape(1, d_model), pool_scale[0].reshape(1, d_model),
        pool_out_w[0].astype(BF16),
        gin, glwt, gkw, gkb, hn, gla_out_w[0].astype(BF16),
        final_norm_w.reshape(1, d_model),
    )
    tile = pl.BlockSpec((1, ts, d_model), lambda b, j: (b, j, 0))
    in_specs = [tile] + [_resident(op.shape) for op in operands[1:]]
    n_sub = ts // SUB_ROWS
    rows = SUB_ROWS + POOL_OFF
    body = functools.partial(_fused_kernel, ts=ts, d_model=d_model, head_k=head_k, head_v=head_v)
    return pl.pallas_call(
        body,
        grid=(batch, seq // ts),
        in_specs=in_specs,
        out_specs=tile,
        out_shape=jax.ShapeDtypeStruct(x.shape, x.dtype),
        scratch_shapes=[
            pltpu.VMEM((n_sub, rows, d_model), F32),
            pltpu.VMEM((n_sub, rows, d_model - gdim), F32),
            pltpu.VMEM((n_sub, rows, d_model - 2 * gdim), F32),
            pltpu.VMEM((n_sub, rows, d_model - 3 * gdim), F32),
            pltpu.VMEM((POOL_HALO, d_model), F32),
            pltpu.VMEM((GLA_HEADS, head_k, head_v), F32),
        ],
        compiler_params=pltpu.CompilerParams(
            dimension_semantics=("arbitrary", "arbitrary"),
            vmem_limit_bytes=V7X_VMEM_LIMIT_BYTES,
        ),
        name="pool_gla_trunk",
    )(*operands)
```

```python
import functools

import jax
import jax.numpy as jnp
from jax import lax
from jax.experimental import pallas as pl
from jax.experimental.pallas import tpu as pltpu

F32 = jnp.float32
BF16 = jnp.bfloat16

RMS_EPS = 1e-6
CHUNK = 64
POOL_WINDOWS = (2, 4, 8, 16)
GLA_HEADS = 4
GLA_GATE_NORMALIZER = 16.0
GLA_GATE_RANK = 16

V7X_LANES = 128
V7X_SUBLANES = 8
V7X_VMEM_LIMIT_BYTES = 56 * 1024 * 1024

SEQ_TILE = 512
SUB_ROWS = 256
TAIL_PIECES = 2
POOL_HALO = max(POOL_WINDOWS)
POOL_OFF = POOL_HALO + V7X_SUBLANES


def _dot(a, b):
    return jnp.dot(a, b, preferred_element_type=F32)


def _dot_nt(a, b):
    return lax.dot_general(a, b, (((1,), (1,)), ((), ())), preferred_element_type=F32)


def _dot_tn(a, b):
    return lax.dot_general(a, b, (((0,), (0,)), ((), ())), preferred_element_type=F32)


def _rms_norm(x, w):
    return x * lax.rsqrt(jnp.mean(x * x, axis=-1, keepdims=True) + RMS_EPS) * w


def _silu(x):
    return x / (1.0 + jnp.exp(-x))


def _log_sigmoid(z):
    return jnp.minimum(z, 0.0) - jnp.log(1.0 + jnp.exp(-jnp.abs(z)))


def _split_bf16(x):
    hi = x.astype(BF16)
    lo = (x - hi.astype(F32)).astype(BF16)
    return hi, lo


def _pool_layer(x, carry, first_frame, w, uext, s1, s2, s3):
    nw_ref, pin_ref, gw_ref, gb_ref, ps_ref, pout_ref = w
    rows, d_model = x.shape
    gdim = d_model // len(POOL_WINDOWS)
    ext = rows + POOL_HALO
    lo_row = V7X_SUBLANES

    normed = _rms_norm(x, nw_ref[0:1, :]).astype(BF16)
    u = _dot(normed, pin_ref[:, :d_model])
    uext[lo_row:POOL_OFF, :] = carry["u_tail"]
    uext[POOL_OFF:POOL_OFF + rows, :] = u
    carry["u_tail"] = u[rows - POOL_HALO:, :]
    yield
    gate = _dot(normed, pin_ref[:, d_model:])

    a1 = uext[lo_row:lo_row + ext, :] + uext[lo_row - 1:lo_row - 1 + ext, :]
    s1[lo_row:lo_row + ext, :] = a1[:, gdim:]
    a2 = s1[lo_row:lo_row + ext, :] + s1[lo_row - 2:lo_row - 2 + ext, :]
    s2[lo_row:lo_row + ext, :] = a2[:, gdim:]
    a3 = s2[lo_row:lo_row + ext, :] + s2[lo_row - 4:lo_row - 4 + ext, :]
    s3[lo_row:lo_row + ext, :] = a3[:, gdim:]
    a4 = s3[lo_row:lo_row + ext, :] + s3[lo_row - 8:lo_row - 8 + ext, :]
    sums = (a1[POOL_HALO:, :gdim], a2[POOL_HALO:, :gdim], a3[POOL_HALO:, :gdim], a4[POOL_HALO:, :])

    frame = first_frame + lax.broadcasted_iota(jnp.int32, (rows, 1), 0) + 1
    ys = []
    for g, win in enumerate(POOL_WINDOWS):
        cols = slice(g * gdim, (g + 1) * gdim)
        inv_count = 1.0 / jnp.minimum(frame, win).astype(F32)
        pooled = sums[g] * inv_count - u[:, cols]
        mixed = _dot(pooled.astype(BF16), gw_ref[g]) + gb_ref[:, cols]
        ys.append(mixed * ps_ref[:, cols] * _silu(gate[:, cols]))
        yield
    y = jnp.concatenate(ys, axis=-1).astype(BF16)
    yield
    return x + _dot(y, pout_ref[...])


def _gla_layer(h, carry, w, head_k, head_v):
    nw_ref, gin_ref, glwt_ref, gkw_ref, gkb_ref, hn_ref, _ = w
    rows, d_model = h.shape
    n_chunks = rows // CHUNK
    key_w = GLA_HEADS * head_k
    heads = range(GLA_HEADS)
    ksl = [slice(hd * head_k, (hd + 1) * head_k) for hd in heads]
    vsl = [slice(hd * head_v, (hd + 1) * head_v) for hd in heads]
    csl = [slice(n * CHUNK, (n + 1) * CHUNK) for n in range(n_chunks)]

    n2 = _rms_norm(h, nw_ref[1:2, :]).astype(BF16)
    glow = _dot_nt(glwt_ref[...], n2).T.astype(BF16)
    qk = _dot(n2, gin_ref[:, :2 * key_w])
    z = _dot(glow, gkw_ref[...]) + gkb_ref[...]
    yield

    vg = _dot(n2, gin_ref[:, 2 * key_w:2 * (key_w + d_model)])
    lg = _log_sigmoid(z) * (1.0 / GLA_GATE_NORMALIZER)
    lg_hi, lg_lo = _split_bf16(lg)
    r = lax.broadcasted_iota(jnp.int32, (rows, rows), 0)
    c = lax.broadcasted_iota(jnp.int32, (rows, rows), 1)
    same_chunk = (r // CHUNK) == (c // CHUNK)
    causal = c <= r
    tril = jnp.where(same_chunk & causal, 1.0, 0.0).astype(BF16)
    cum = _dot(tril, lg_hi) + _dot(tril, lg_lo)
    yield

    totals = [cum[cs.stop - 1:cs.stop, :] for cs in csl]
    tot = jnp.concatenate([jnp.broadcast_to(t, (CHUNK, key_w)) for t in totals], axis=0)
    e_pos = jnp.exp(cum)
    e_neg = jnp.exp(-cum)
    q = qk[:, :key_w] * (head_k ** -0.5)
    k = qk[:, key_w:]
    q_pos = (q * e_pos).astype(BF16)
    q_neg = (q * e_neg).astype(BF16)
    k_neg = (k * e_neg).astype(BF16)
    k_pos = (k * e_pos).astype(BF16)
    k_dec = (k * jnp.exp(tot - cum)).astype(BF16)
    v = vg[:, :d_model].astype(BF16)
    gate2 = vg[:, d_model:]
    decay_t = [[jnp.broadcast_to(jnp.exp(t[:, ks]), (V7X_LANES, head_k)).T for t in totals]
               for ks in ksl]
    yield

    fwd = [_dot_nt(q_pos[:, ks], k_neg[:, ks]) for ks in ksl]
    bwd = [_dot_nt(q_neg[:, ks], k_pos[:, ks]) for ks in ksl]
    upd = [[_dot_tn(k_dec[cs, ks], v[cs, vs]) for cs in csl]
           for ks, vs in zip(ksl, vsl)]
    yield

    scores = [jnp.where(same_chunk, jnp.where(causal, fwd[hd], bwd[hd]), 0.0).astype(BF16)
              for hd in heads]
    entering = []
    for hd in heads:
        st = carry["states"][hd]
        ent_h = []
        for n in range(n_chunks):
            ent_h.append(st.astype(BF16))
            d_n = jnp.concatenate([decay_t[hd][n]] * (head_v // V7X_LANES), axis=-1)
            st = st * d_n + upd[hd][n]
        carry["states"][hd] = st
        entering.append(ent_h)
        yield

    outs = []
    for hd in heads:
        inter = [_dot(q_pos[cs, ksl[hd]], entering[hd][n]) for n, cs in enumerate(csl)]
        o_h = _dot(scores[hd], v[:, vsl[hd]]) + jnp.concatenate(inter, axis=0)
        o_h = o_h * lax.rsqrt(jnp.mean(o_h * o_h, axis=-1, keepdims=True) + RMS_EPS) * hn_ref[:, vsl[hd]]
        outs.append(o_h)
        yield
    return (jnp.concatenate(outs, axis=-1) * _silu(gate2)).astype(BF16)


def _sub_block(x_ref, o_ref, rows, first_frame, carry, pool_w, gla_w, fnw_ref, pool_scratch,
               head_k, head_v):
    h = yield from _pool_layer(x_ref[0, rows, :], carry, first_frame, pool_w, *pool_scratch)
    yield
    y2 = yield from _gla_layer(h, carry, gla_w, head_k, head_v)
    n_rows = rows.stop - rows.start
    piece = n_rows // TAIL_PIECES
    for p in range(TAIL_PIECES):
        part = slice(p * piece, (p + 1) * piece)
        h2 = h[part, :] + _dot(y2[part, :], gla_w[-1][...])
        o_ref[0, rows.start + p * piece:rows.start + (p + 1) * piece, :] = _rms_norm(h2, fnw_ref[...])


def _fused_kernel(x_ref, nw_ref, pin_ref, gw_ref, gb_ref, ps_ref, pout_ref,
                  gin_ref, glwt_ref, gkw_ref, gkb_ref, hn_ref, gout_ref, fnw_ref,
                  o_ref,
                  uext, s1, s2, s3, halo, state,
                  *, ts, d_model, head_k, head_v):
    j = pl.program_id(1)
    n_sub = ts // SUB_ROWS
    lo_row = V7X_SUBLANES

    @pl.when(jnp.logical_and(pl.program_id(0) == 0, j == 0))
    def _():
        for buf in (uext, s1, s2, s3):
            buf[:, 0:lo_row, :] = jnp.zeros((n_sub, lo_row, buf.shape[2]), F32)

    @pl.when(j == 0)
    def _():
        halo[...] = jnp.zeros_like(halo)
        state[...] = jnp.zeros_like(state)

    pool_w = (nw_ref, pin_ref, gw_ref, gb_ref, ps_ref, pout_ref)
    gla_w = (nw_ref, gin_ref, glwt_ref, gkw_ref, gkb_ref, hn_ref, gout_ref)

    carry = {"u_tail": halo[...], "states": [state[hd] for hd in range(GLA_HEADS)]}
    blocks = [
        _sub_block(x_ref, o_ref, slice(s * SUB_ROWS, (s + 1) * SUB_ROWS), j * ts + s * SUB_ROWS,
                   carry, pool_w, gla_w, fnw_ref, (uext.at[s], s1.at[s], s2.at[s], s3.at[s]),
                   head_k, head_v)
        for s in range(n_sub)
    ]
    while blocks:
        for blk in list(blocks):
            try:
                next(blk)
            except StopIteration:
                blocks.remove(blk)
    halo[...] = carry["u_tail"]
    for hd in range(GLA_HEADS):
        state[hd] = carry["states"][hd]


def _resident(shape):
    zeros = (0,) * len(shape)
    return pl.BlockSpec(shape, lambda b, j: zeros, pipeline_mode=pl.Buffered(1))


def kernel(x, norm_w, pool_in_w, pool_group_w, pool_group_b, pool_scale, pool_out_w,
           gla_in_w, gla_gk_w, gla_gk_b, gla_head_norm_w, gla_out_w, final_norm_w):
    batch, seq, d_model = x.shape
    ts = SEQ_TILE
    assert seq % ts == 0 and ts % SUB_ROWS == 0 and SUB_ROWS % CHUNK == 0
    assert norm_w.shape[0] == 2 and pool_in_w.shape[0] == 1 and gla_in_w.shape[0] == 1
    key_w = gla_gk_w.shape[-1]
    head_k = key_w // GLA_HEADS
    head_v = d_model // GLA_HEADS
    n_groups = len(POOL_WINDOWS)
    gdim = d_model // n_groups
    qkvg_w = 2 * key_w + 2 * d_model
    assert gla_in_w.shape[-1] == qkvg_w + GLA_GATE_RANK

    rank_pad = V7X_LANES - GLA_GATE_RANK
    gin = gla_in_w[0].astype(BF16)
    glwt = jnp.pad(gla_in_w[0, :, qkvg_w:].T, ((0, rank_pad), (0, 0))).astype(BF16)
    gkw = jnp.pad(gla_gk_w[0], ((0, rank_pad), (0, 0))).astype(BF16)
    gkb = gla_gk_b[0].reshape(1, key_w)
    hn = jnp.tile(gla_head_norm_w[0], GLA_HEADS).reshape(1, d_model)

    operands = (
        x, norm_w, pool_in_w[0].astype(BF16), pool_group_w[0].astype(BF16),
        pool_group_b[0].reshape(1, d_model), pool_scale[0].reshape(1, d_model),
        pool_out_w[0].astype(BF16),
        gin, glwt, gkw, gkb, hn, gla_out_w[0].astype(BF16),
        final_norm_w.reshape(1, d_model),
    )
    tile = pl.BlockSpec((1, ts, d_model), lambda b, j: (b, j, 0))
    in_specs = [tile] + [_resident(op.shape) for op in operands[1:]]
    n_sub = ts // SUB_ROWS
    rows = SUB_ROWS + POOL_OFF
    body = functools.partial(_fused_kernel, ts=ts, d_model=d_model, head_k=head_k, head_v=head_v)
    return pl.pallas_call(
        body,
        grid=(batch, seq // ts),
        in_specs=in_specs,
        out_specs=tile,
        out_shape=jax.ShapeDtypeStruct(x.shape, x.dtype),
        scratch_shapes=[
            pltpu.VMEM((n_sub, rows, d_model), F32),
            pltpu.VMEM((n_sub, rows, d_model - gdim), F32),
            pltpu.VMEM((n_sub, rows, d_model - 2 * gdim), F32),
            pltpu.VMEM((n_sub, rows, d_model - 3 * gdim), F32),
            pltpu.VMEM((POOL_HALO, d_model), F32),
            pltpu.VMEM((GLA_HEADS, head_k, head_v), F32),
        ],
        compiler_params=pltpu.CompilerParams(
            dimension_semantics=("arbitrary", "arbitrary"),
            vmem_limit_bytes=V7X_VMEM_LIMIT_BYTES,
        ),
        name="pool_gla_trunk",
    )(*operands)
```

```python
import functools

import jax
import jax.numpy as jnp
from jax import lax
from jax.experimental import pallas as pl
from jax.experimental.pallas import tpu as pltpu

F32 = jnp.float32
BF16 = jnp.bfloat16

RMS_EPS = 1e-6
CHUNK = 64
POOL_WINDOWS = (2, 4, 8, 16)
GLA_HEADS = 4
GLA_GATE_NORMALIZER = 16.0
GLA_GATE_RANK = 16

V7X_LANES = 128
V7X_SUBLANES = 8
V7X_VMEM_LIMIT_BYTES = 56 * 1024 * 1024

SEQ_TILE = 512
SUB_ROWS = 256
TAIL_PIECES = 2
POOL_HALO = max(POOL_WINDOWS)
POOL_OFF = POOL_HALO + V7X_SUBLANES


def _dot(a, b):
    return jnp.dot(a, b, preferred_element_type=F32)


def _dot_nt(a, b):
    return lax.dot_general(a, b, (((1,), (1,)), ((), ())), preferred_element_type=F32)


def _dot_tn(a, b):
    return lax.dot_general(a, b, (((0,), (0,)), ((), ())), preferred_element_type=F32)


def _rms_norm(x, w):
    return x * lax.rsqrt(jnp.mean(x * x, axis=-1, keepdims=True) + RMS_EPS) * w


def _silu(x):
    return x / (1.0 + jnp.exp(-x))


def _log_sigmoid(z):
    return jnp.minimum(z, 0.0) - jnp.log(1.0 + jnp.exp(-jnp.abs(z)))


def _split_bf16(x):
    hi = x.astype(BF16)
    lo = (x - hi.astype(F32)).astype(BF16)
    return hi, lo


def _pool_layer(x, carry, first_frame, w, uext, s1, s2, s3):
    nw_ref, pin_ref, gw_ref, gb_ref, ps_ref, pout_ref = w
    rows, d_model = x.shape
    gdim = d_model // len(POOL_WINDOWS)
    ext = rows + POOL_HALO
    lo_row = V7X_SUBLANES

    normed = _rms_norm(x, nw_ref[0:1, :]).astype(BF16)
    u = _dot(normed, pin_ref[:, :d_model])
    uext[lo_row:POOL_OFF, :] = carry["u_tail"]
    uext[POOL_OFF:POOL_OFF + rows, :] = u
    carry["u_tail"] = u[rows - POOL_HALO:, :]
    yield
    gate = _dot(normed, pin_ref[:, d_model:])

    a1 = uext[lo_row:lo_row + ext, :] + uext[lo_row - 1:lo_row - 1 + ext, :]
    s1[lo_row:lo_row + ext, :] = a1[:, gdim:]
    a2 = s1[lo_row:lo_row + ext, :] + s1[lo_row - 2:lo_row - 2 + ext, :]
    s2[lo_row:lo_row + ext, :] = a2[:, gdim:]
    a3 = s2[lo_row:lo_row + ext, :] + s2[lo_row - 4:lo_row - 4 + ext, :]
    s3[lo_row:lo_row + ext, :] = a3[:, gdim:]
    a4 = s3[lo_row:lo_row + ext, :] + s3[lo_row - 8:lo_row - 8 + ext, :]
    sums = (a1[POOL_HALO:, :gdim], a2[POOL_HALO:, :gdim], a3[POOL_HALO:, :gdim], a4[POOL_HALO:, :])

    frame = first_frame + lax.broadcasted_iota(jnp.int32, (rows, 1), 0) + 1
    ys = []
    for g, win in enumerate(POOL_WINDOWS):
        cols = slice(g * gdim, (g + 1) * gdim)
        inv_count = 1.0 / jnp.minimum(frame, win).astype(F32)
        pooled = sums[g] * inv_count - u[:, cols]
        mixed = _dot(pooled.astype(BF16), gw_ref[g]) + gb_ref[:, cols]
        ys.append(mixed * ps_ref[:, cols] * _silu(gate[:, cols]))
        yield
    y = jnp.concatenate(ys, axis=-1).astype(BF16)
    yield
    return x + _dot(y, pout_ref[...])


def _gla_layer(h, carry, w, head_k, head_v):
    nw_ref, gint_ref, gkw_ref, gkb_ref, hn_ref, _ = w
    rows, d_model = h.shape
    n_chunks = rows // CHUNK
    key_w = GLA_HEADS * head_k
    heads = range(GLA_HEADS)
    ksl = [slice(hd * head_k, (hd + 1) * head_k) for hd in heads]
    vsl = [slice(hd * head_v, (hd + 1) * head_v) for hd in heads]
    csl = [slice(n * CHUNK, (n + 1) * CHUNK) for n in range(n_chunks)]

    n2 = _rms_norm(h, nw_ref[1:2, :]).astype(BF16)
    qkvg_w = 2 * (key_w + d_model)
    glow = _dot_nt(gint_ref[qkvg_w:, :], n2).T.astype(BF16)
    qk = _dot_nt(n2, gint_ref[:2 * key_w, :])
    z = _dot(glow, gkw_ref[...]) + gkb_ref[...]
    yield

    vg = _dot_nt(n2, gint_ref[2 * key_w:qkvg_w, :])
    lg = _log_sigmoid(z) * (1.0 / GLA_GATE_NORMALIZER)
    lg_hi, lg_lo = _split_bf16(lg)
    r = lax.broadcasted_iota(jnp.int32, (rows, rows), 0)
    c = lax.broadcasted_iota(jnp.int32, (rows, rows), 1)
    same_chunk = (r // CHUNK) == (c // CHUNK)
    causal = c <= r
    tril = jnp.where(same_chunk & causal, 1.0, 0.0).astype(BF16)
    cum = _dot(tril, lg_hi) + _dot(tril, lg_lo)
    yield

    totals = [cum[cs.stop - 1:cs.stop, :] for cs in csl]
    tot = jnp.concatenate([jnp.broadcast_to(t, (CHUNK, key_w)) for t in totals], axis=0)
    e_pos = jnp.exp(cum)
    e_neg = jnp.exp(-cum)
    q = qk[:, :key_w] * (head_k ** -0.5)
    k = qk[:, key_w:]
    q_pos = (q * e_pos).astype(BF16)
    q_neg = (q * e_neg).astype(BF16)
    k_neg = (k * e_neg).astype(BF16)
    k_pos = (k * e_pos).astype(BF16)
    k_dec = (k * jnp.exp(tot - cum)).astype(BF16)
    v = vg[:, :d_model].astype(BF16)
    gate2 = vg[:, d_model:]
    decay_t = [[jnp.broadcast_to(jnp.exp(t[:, ks]), (V7X_LANES, head_k)).T for t in totals]
               for ks in ksl]
    yield

    fwd = [_dot_nt(q_pos[:, ks], k_neg[:, ks]) for ks in ksl]
    bwd = [_dot_nt(q_neg[:, ks], k_pos[:, ks]) for ks in ksl]
    upd = [[_dot_tn(k_dec[cs, ks], v[cs, vs]) for cs in csl]
           for ks, vs in zip(ksl, vsl)]
    yield

    scores = [jnp.where(same_chunk, jnp.where(causal, fwd[hd], bwd[hd]), 0.0).astype(BF16)
              for hd in heads]
    entering = []
    for hd in heads:
        st = carry["states"][hd]
        ent_h = []
        for n in range(n_chunks):
            ent_h.append(st.astype(BF16))
            d_n = jnp.concatenate([decay_t[hd][n]] * (head_v // V7X_LANES), axis=-1)
            st = st * d_n + upd[hd][n]
        carry["states"][hd] = st
        entering.append(ent_h)
        yield

    outs = []
    for hd in heads:
        inter = [_dot(q_pos[cs, ksl[hd]], entering[hd][n]) for n, cs in enumerate(csl)]
        o_h = _dot(scores[hd], v[:, vsl[hd]]) + jnp.concatenate(inter, axis=0)
        o_h = o_h * lax.rsqrt(jnp.mean(o_h * o_h, axis=-1, keepdims=True) + RMS_EPS) * hn_ref[:, vsl[hd]]
        outs.append(o_h)
        yield
    return (jnp.concatenate(outs, axis=-1) * _silu(gate2)).astype(BF16)


def _sub_block(x_ref, o_ref, rows, first_frame, carry, pool_w, gla_w, fnw_ref, pool_scratch,
               head_k, head_v):
    h = yield from _pool_layer(x_ref[0, rows, :], carry, first_frame, pool_w, *pool_scratch)
    yield
    y2 = yield from _gla_layer(h, carry, gla_w, head_k, head_v)
    n_rows = rows.stop - rows.start
    piece = n_rows // TAIL_PIECES
    for p in range(TAIL_PIECES):
        part = slice(p * piece, (p + 1) * piece)
        h2 = h[part, :] + _dot(y2[part, :], gla_w[-1][...])
        o_ref[0, rows.start + p * piece:rows.start + (p + 1) * piece, :] = _rms_norm(h2, fnw_ref[...])


def _fused_kernel(x_ref, nw_ref, pin_ref, gw_ref, gb_ref, ps_ref, pout_ref,
                  gint_ref, gkw_ref, gkb_ref, hn_ref, gout_ref, fnw_ref,
                  o_ref,
                  uext, s1, s2, s3, halo, state,
                  *, ts, d_model, head_k, head_v):
    j = pl.program_id(1)
    n_sub = ts // SUB_ROWS
    lo_row = V7X_SUBLANES

    @pl.when(jnp.logical_and(pl.program_id(0) == 0, j == 0))
    def _():
        for buf in (uext, s1, s2, s3):
            buf[:, 0:lo_row, :] = jnp.zeros((n_sub, lo_row, buf.shape[2]), F32)

    @pl.when(j == 0)
    def _():
        halo[...] = jnp.zeros_like(halo)
        state[...] = jnp.zeros_like(state)

    pool_w = (nw_ref, pin_ref, gw_ref, gb_ref, ps_ref, pout_ref)
    gla_w = (nw_ref, gint_ref, gkw_ref, gkb_ref, hn_ref, gout_ref)

    carry = {"u_tail": halo[...], "states": [state[hd] for hd in range(GLA_HEADS)]}
    blocks = [
        _sub_block(x_ref, o_ref, slice(s * SUB_ROWS, (s + 1) * SUB_ROWS), j * ts + s * SUB_ROWS,
                   carry, pool_w, gla_w, fnw_ref, (uext.at[s], s1.at[s], s2.at[s], s3.at[s]),
                   head_k, head_v)
        for s in range(n_sub)
    ]
    while blocks:
        for blk in list(blocks):
            try:
                next(blk)
            except StopIteration:
                blocks.remove(blk)
    halo[...] = carry["u_tail"]
    for hd in range(GLA_HEADS):
        state[hd] = carry["states"][hd]


def _resident(shape):
    zeros = (0,) * len(shape)
    return pl.BlockSpec(shape, lambda b, j: zeros, pipeline_mode=pl.Buffered(1))


def kernel(x, norm_w, pool_in_w, pool_group_w, pool_group_b, pool_scale, pool_out_w,
           gla_in_w, gla_gk_w, gla_gk_b, gla_head_norm_w, gla_out_w, final_norm_w):
    batch, seq, d_model = x.shape
    ts = SEQ_TILE
    assert seq % ts == 0 and ts % SUB_ROWS == 0 and SUB_ROWS % CHUNK == 0
    assert norm_w.shape[0] == 2 and pool_in_w.shape[0] == 1 and gla_in_w.shape[0] == 1
    key_w = gla_gk_w.shape[-1]
    head_k = key_w // GLA_HEADS
    head_v = d_model // GLA_HEADS
    n_groups = len(POOL_WINDOWS)
    gdim = d_model // n_groups
    qkvg_w = 2 * key_w + 2 * d_model
    assert gla_in_w.shape[-1] == qkvg_w + GLA_GATE_RANK

    rank_pad = V7X_LANES - GLA_GATE_RANK
    gint = jnp.pad(gla_in_w[0].T, ((0, rank_pad), (0, 0))).astype(BF16)
    gkw = jnp.pad(gla_gk_w[0], ((0, rank_pad), (0, 0))).astype(BF16)
    gkb = gla_gk_b[0].reshape(1, key_w)
    hn = jnp.tile(gla_head_norm_w[0], GLA_HEADS).reshape(1, d_model)

    operands = (
        x, norm_w, pool_in_w[0].astype(BF16), pool_group_w[0].astype(BF16),
        pool_group_b[0].reshape(1, d_model), pool_scale[0].reshape(1, d_model),
        pool_out_w[0].astype(BF16),
        gint, gkw, gkb, hn, gla_out_w[0].astype(BF16),
        final_norm_w.reshape(1, d_model),
    )
    tile = pl.BlockSpec((1, ts, d_model), lambda b, j: (b, j, 0))
    in_specs = [tile] + [_resident(op.shape) for op in operands[1:]]
    n_sub = ts // SUB_ROWS
    rows = SUB_ROWS + POOL_OFF
    body = functools.partial(_fused_kernel, ts=ts, d_model=d_model, head_k=head_k, head_v=head_v)
    return pl.pallas_call(
        body,
        grid=(batch, seq // ts),
        in_specs=in_specs,
        out_specs=tile,
        out_shape=jax.ShapeDtypeStruct(x.shape, x.dtype),
        scratch_shapes=[
            pltpu.VMEM((n_sub, rows, d_model), F32),
            pltpu.VMEM((n_sub, rows, d_model - gdim), F32),
            pltpu.VMEM((n_sub, rows, d_model - 2 * gdim), F32),
            pltpu.VMEM((n_sub, rows, d_model - 3 * gdim), F32),
            pltpu.VMEM((POOL_HALO, d_model), F32),
            pltpu.VMEM((GLA_HEADS, head_k, head_v), F32),
        ],
        compiler_params=pltpu.CompilerParams(
            dimension_semantics=("arbitrary", "arbitrary"),
            vmem_limit_bytes=V7X_VMEM_LIMIT_BYTES,
        ),
        name="pool_gla_trunk",
    )(*operands)
```

```python
import functools

import jax
import jax.numpy as jnp
from jax import lax
from jax.experimental import pallas as pl
from jax.experimental.pallas import tpu as pltpu

F32 = jnp.float32
BF16 = jnp.bfloat16

RMS_EPS = 1e-6
CHUNK = 64
POOL_WINDOWS = (2, 4, 8, 16)
GLA_HEADS = 4
GLA_GATE_NORMALIZER = 16.0
GLA_GATE_RANK = 16

V7X_LANES = 128
V7X_SUBLANES = 8
V7X_VMEM_LIMIT_BYTES = 56 * 1024 * 1024

SEQ_TILE = 512
SUB_ROWS = 256
TAIL_PIECES = 2
WEIGHT_ROWS = 256
WEIGHT_COLS = 2048
POOL_HALO = max(POOL_WINDOWS)
POOL_OFF = POOL_HALO + V7X_SUBLANES


def _dot(a, b):
    return jnp.dot(a, b, preferred_element_type=F32)


def _dot_nt(a, b):
    return lax.dot_general(a, b, (((1,), (1,)), ((), ())), preferred_element_type=F32)


def _dot_tn(a, b):
    return lax.dot_general(a, b, (((0,), (0,)), ((), ())), preferred_element_type=F32)


def _rms_norm(x, w):
    return x * lax.rsqrt(jnp.mean(x * x, axis=-1, keepdims=True) + RMS_EPS) * w


def _silu(x):
    return x / (1.0 + jnp.exp(-x))


def _log_sigmoid(z):
    return jnp.minimum(z, 0.0) - jnp.log(1.0 + jnp.exp(-jnp.abs(z)))


def _split_bf16(x):
    hi = x.astype(BF16)
    lo = (x - hi.astype(F32)).astype(BF16)
    return hi, lo


def _pool_layer(x, carry, first_frame, w, uext, s1, s2, s3):
    nw_ref, pin_ref, gw_ref, gb_ref, ps_ref, pout_ref = w
    rows, d_model = x.shape
    gdim = d_model // len(POOL_WINDOWS)
    ext = rows + POOL_HALO
    lo_row = V7X_SUBLANES

    normed = _rms_norm(x, nw_ref[0:1, :]).astype(BF16)
    u = _dot(normed, pin_ref[:, :d_model])
    uext[lo_row:POOL_OFF, :] = carry["u_tail"]
    uext[POOL_OFF:POOL_OFF + rows, :] = u
    carry["u_tail"] = u[rows - POOL_HALO:, :]
    yield
    gate = _dot(normed, pin_ref[:, d_model:])

    a1 = uext[lo_row:lo_row + ext, :] + uext[lo_row - 1:lo_row - 1 + ext, :]
    s1[lo_row:lo_row + ext, :] = a1[:, gdim:]
    a2 = s1[lo_row:lo_row + ext, :] + s1[lo_row - 2:lo_row - 2 + ext, :]
    s2[lo_row:lo_row + ext, :] = a2[:, gdim:]
    a3 = s2[lo_row:lo_row + ext, :] + s2[lo_row - 4:lo_row - 4 + ext, :]
    s3[lo_row:lo_row + ext, :] = a3[:, gdim:]
    a4 = s3[lo_row:lo_row + ext, :] + s3[lo_row - 8:lo_row - 8 + ext, :]
    sums = (a1[POOL_HALO:, :gdim], a2[POOL_HALO:, :gdim], a3[POOL_HALO:, :gdim], a4[POOL_HALO:, :])

    frame = first_frame + lax.broadcasted_iota(jnp.int32, (rows, 1), 0) + 1
    ys = []
    for g, win in enumerate(POOL_WINDOWS):
        cols = slice(g * gdim, (g + 1) * gdim)
        inv_count = 1.0 / jnp.minimum(frame, win).astype(F32)
        pooled = sums[g] * inv_count - u[:, cols]
        mixed = _dot(pooled.astype(BF16), gw_ref[cols, :]) + gb_ref[:, cols]
        ys.append(mixed * ps_ref[:, cols] * _silu(gate[:, cols]))
        yield
    y = jnp.concatenate(ys, axis=-1).astype(BF16)
    yield
    return x + _dot(y, pout_ref[...])


def _gla_layer(h, carry, w, head_k, head_v):
    nw_ref, gint_ref, gkw_ref, gkb_ref, hn_ref, _ = w
    rows, d_model = h.shape
    n_chunks = rows // CHUNK
    key_w = GLA_HEADS * head_k
    heads = range(GLA_HEADS)
    ksl = [slice(hd * head_k, (hd + 1) * head_k) for hd in heads]
    vsl = [slice(hd * head_v, (hd + 1) * head_v) for hd in heads]
    csl = [slice(n * CHUNK, (n + 1) * CHUNK) for n in range(n_chunks)]

    n2 = _rms_norm(h, nw_ref[1:2, :]).astype(BF16)
    qkvg_w = 2 * (key_w + d_model)
    glow = _dot_nt(gint_ref[qkvg_w:, :], n2).T.astype(BF16)
    qk = _dot_nt(n2, gint_ref[:2 * key_w, :])
    z = _dot(glow, gkw_ref[...]) + gkb_ref[...]
    yield

    vg = _dot_nt(n2, gint_ref[2 * key_w:qkvg_w, :])
    lg = _log_sigmoid(z) * (1.0 / GLA_GATE_NORMALIZER)
    lg_hi, lg_lo = _split_bf16(lg)
    r = lax.broadcasted_iota(jnp.int32, (rows, rows), 0)
    c = lax.broadcasted_iota(jnp.int32, (rows, rows), 1)
    same_chunk = (r // CHUNK) == (c // CHUNK)
    causal = c <= r
    tril = jnp.where(same_chunk & causal, 1.0, 0.0).astype(BF16)
    cum = _dot(tril, lg_hi) + _dot(tril, lg_lo)
    yield

    totals = [cum[cs.stop - 1:cs.stop, :] for cs in csl]
    tot = jnp.concatenate([jnp.broadcast_to(t, (CHUNK, key_w)) for t in totals], axis=0)
    e_pos = jnp.exp(cum)
    e_neg = jnp.exp(-cum)
    q = qk[:, :key_w] * (head_k ** -0.5)
    k = qk[:, key_w:]
    q_pos = (q * e_pos).astype(BF16)
    q_neg = (q * e_neg).astype(BF16)
    k_neg = (k * e_neg).astype(BF16)
    k_pos = (k * e_pos).astype(BF16)
    k_dec = (k * jnp.exp(tot - cum)).astype(BF16)
    v = vg[:, :d_model].astype(BF16)
    gate_act = _silu(vg[:, d_model:])
    decay_t = [[jnp.broadcast_to(jnp.exp(t[:, ks]), (V7X_LANES, head_k)).T for t in totals]
               for ks in ksl]
    yield

    fwd = [_dot_nt(q_pos[:, ks], k_neg[:, ks]) for ks in ksl]
    bwd = [_dot_nt(q_neg[:, ks], k_pos[:, ks]) for ks in ksl]
    upd = [[_dot_tn(k_dec[cs, ks], v[cs, vs]) for cs in csl]
           for ks, vs in zip(ksl, vsl)]
    yield

    scores = [jnp.where(same_chunk, jnp.where(causal, fwd[hd], bwd[hd]), 0.0).astype(BF16)
              for hd in heads]
    entering = []
    for hd in heads:
        st = carry["states"][hd]
        ent_h = []
        for n in range(n_chunks):
            ent_h.append(st.astype(BF16))
            d_n = jnp.concatenate([decay_t[hd][n]] * (head_v // V7X_LANES), axis=-1)
            st = st * d_n + upd[hd][n]
        carry["states"][hd] = st
        entering.append(ent_h)
        yield

    outs = []
    for hd in heads:
        inter = [_dot(q_pos[cs, ksl[hd]], entering[hd][n]) for n, cs in enumerate(csl)]
        o_h = _dot(scores[hd], v[:, vsl[hd]]) + jnp.concatenate(inter, axis=0)
        o_h = o_h * lax.rsqrt(jnp.mean(o_h * o_h, axis=-1, keepdims=True) + RMS_EPS) * hn_ref[:, vsl[hd]]
        outs.append(o_h)
        yield
    return (jnp.concatenate(outs, axis=-1) * gate_act).astype(BF16)


def _sub_block(x_ref, o_ref, rows, first_frame, carry, pool_w, gla_w, fnw_ref, pool_scratch,
               head_k, head_v):
    h = yield from _pool_layer(x_ref[0, rows, :], carry, first_frame, pool_w, *pool_scratch)
    yield
    y2 = yield from _gla_layer(h, carry, gla_w, head_k, head_v)
    n_rows = rows.stop - rows.start
    piece = n_rows // TAIL_PIECES
    for p in range(TAIL_PIECES):
        part = slice(p * piece, (p + 1) * piece)
        h2 = h[part, :] + _dot(y2[part, :], gla_w[-1][...])
        o_ref[0, rows.start + p * piece:rows.start + (p + 1) * piece, :] = _rms_norm(h2, fnw_ref[...])


def _load_weights(pairs, stage, sems):
    chunks = []
    for src, dst in pairs:
        n_rows, n_cols = src.shape
        for r0 in range(0, n_rows, WEIGHT_ROWS):
            chunks.append((src, dst, r0, min(WEIGHT_ROWS, n_rows - r0), n_cols))

    def copy(i):
        src, _, r0, n, c = chunks[i]
        slot = i % 2
        return pltpu.make_async_copy(src.at[r0:r0 + n, :], stage.at[slot, 0:n, 0:c], sems.at[slot])

    copy(0).start()
    for i, (_, dst, r0, n, c) in enumerate(chunks):
        if i + 1 < len(chunks):
            copy(i + 1).start()
        copy(i).wait()
        dst[r0:r0 + n, :] = stage[i % 2, 0:n, 0:c].astype(BF16)


def _fused_kernel(x_ref, nw_ref, gb_ref, ps_ref, gkw_ref, gkb_ref, hn_ref, fnw_ref,
                  pin_hbm, gw_hbm, pout_hbm, gint_hbm, gout_hbm,
                  o_ref,
                  pin_ref, gw_ref, pout_ref, gint_ref, gout_ref, stage, sems,
                  uext, s1, s2, s3, halo, state,
                  *, ts, d_model, head_k, head_v):
    j = pl.program_id(1)
    n_sub = ts // SUB_ROWS
    lo_row = V7X_SUBLANES

    @pl.when(jnp.logical_and(pl.program_id(0) == 0, j == 0))
    def _():
        _load_weights([(pin_hbm, pin_ref), (gw_hbm, gw_ref), (pout_hbm, pout_ref),
                       (gint_hbm, gint_ref), (gout_hbm, gout_ref)], stage, sems)
        pad_rows = gint_ref.shape[0] - gint_hbm.shape[0]
        gint_ref[gint_hbm.shape[0]:, :] = jnp.zeros((pad_rows, gint_ref.shape[1]), BF16)
        for buf in (uext, s1, s2, s3):
            buf[:, 0:lo_row, :] = jnp.zeros((n_sub, lo_row, buf.shape[2]), F32)

    @pl.when(j == 0)
    def _():
        halo[...] = jnp.zeros_like(halo)
        state[...] = jnp.zeros_like(state)

    pool_w = (nw_ref, pin_ref, gw_ref, gb_ref, ps_ref, pout_ref)
    gla_w = (nw_ref, gint_ref, gkw_ref, gkb_ref, hn_ref, gout_ref)

    carry = {"u_tail": halo[...], "states": [state[hd] for hd in range(GLA_HEADS)]}
    blocks = [
        _sub_block(x_ref, o_ref, slice(s * SUB_ROWS, (s + 1) * SUB_ROWS), j * ts + s * SUB_ROWS,
                   carry, pool_w, gla_w, fnw_ref, (uext.at[s], s1.at[s], s2.at[s], s3.at[s]),
                   head_k, head_v)
        for s in range(n_sub)
    ]
    while blocks:
        for blk in list(blocks):
            try:
                next(blk)
            except StopIteration:
                blocks.remove(blk)
    halo[...] = carry["u_tail"]
    for hd in range(GLA_HEADS):
        state[hd] = carry["states"][hd]


def _resident(shape):
    zeros = (0,) * len(shape)
    return pl.BlockSpec(shape, lambda b, j: zeros, pipeline_mode=pl.Buffered(1))


def kernel(x, norm_w, pool_in_w, pool_group_w, pool_group_b, pool_scale, pool_out_w,
           gla_in_w, gla_gk_w, gla_gk_b, gla_head_norm_w, gla_out_w, final_norm_w):
    batch, seq, d_model = x.shape
    ts = SEQ_TILE
    assert seq % ts == 0 and ts % SUB_ROWS == 0 and SUB_ROWS % CHUNK == 0
    assert norm_w.shape[0] == 2 and pool_in_w.shape[0] == 1 and gla_in_w.shape[0] == 1
    key_w = gla_gk_w.shape[-1]
    head_k = key_w // GLA_HEADS
    head_v = d_model // GLA_HEADS
    n_groups = len(POOL_WINDOWS)
    gdim = d_model // n_groups
    qkvg_w = 2 * key_w + 2 * d_model
    assert gla_in_w.shape[-1] == qkvg_w + GLA_GATE_RANK

    rank_pad = V7X_LANES - GLA_GATE_RANK
    gkw = jnp.pad(gla_gk_w[0], ((0, rank_pad), (0, 0))).astype(BF16)
    small = (norm_w, pool_group_b[0].reshape(1, d_model), pool_scale[0].reshape(1, d_model),
             gkw, gla_gk_b[0].reshape(1, key_w),
             jnp.tile(gla_head_norm_w[0], GLA_HEADS).reshape(1, d_model),
             final_norm_w.reshape(1, d_model))
    big = (pool_in_w[0], pool_group_w[0].reshape(d_model, gdim), pool_out_w[0],
           gla_in_w[0].T, gla_out_w[0])
    assert all(w.shape[1] <= WEIGHT_COLS for w in big)

    operands = (x,) + small + big
    tile = pl.BlockSpec((1, ts, d_model), lambda b, j: (b, j, 0))
    in_specs = ([tile] + [_resident(op.shape) for op in small]
                + [pl.BlockSpec(memory_space=pl.ANY)] * len(big))
    n_sub = ts // SUB_ROWS
    rows = SUB_ROWS + POOL_OFF
    body = functools.partial(_fused_kernel, ts=ts, d_model=d_model, head_k=head_k, head_v=head_v)
    return pl.pallas_call(
        body,
        grid=(batch, seq // ts),
        in_specs=in_specs,
        out_specs=tile,
        out_shape=jax.ShapeDtypeStruct(x.shape, x.dtype),
        scratch_shapes=[
            pltpu.VMEM(pool_in_w.shape[1:], BF16),
            pltpu.VMEM((d_model, gdim), BF16),
            pltpu.VMEM(pool_out_w.shape[1:], BF16),
            pltpu.VMEM((qkvg_w + V7X_LANES, d_model), BF16),
            pltpu.VMEM(gla_out_w.shape[1:], BF16),
            pltpu.VMEM((2, WEIGHT_ROWS, WEIGHT_COLS), F32),
            pltpu.SemaphoreType.DMA((2,)),
            pltpu.VMEM((n_sub, rows, d_model), F32),
            pltpu.VMEM((n_sub, rows, d_model - gdim), F32),
            pltpu.VMEM((n_sub, rows, d_model - 2 * gdim), F32),
            pltpu.VMEM((n_sub, rows, d_model - 3 * gdim), F32),
            pltpu.VMEM((POOL_HALO, d_model), F32),
            pltpu.VMEM((GLA_HEADS, head_k, head_v), F32),
        ],
        compiler_params=pltpu.CompilerParams(
            dimension_semantics=("arbitrary", "arbitrary"),
            vmem_limit_bytes=V7X_VMEM_LIMIT_BYTES,
        ),
        name="pool_gla_trunk",
    )(*operands)
```

```python
import functools

import jax
import jax.numpy as jnp
from jax import lax
from jax.experimental import pallas as pl
from jax.experimental.pallas import tpu as pltpu

F32 = jnp.float32
BF16 = jnp.bfloat16

RMS_EPS = 1e-6
CHUNK = 64
POOL_WINDOWS = (2, 4, 8, 16)
GLA_HEADS = 4
GLA_GATE_NORMALIZER = 16.0
GLA_GATE_RANK = 16

V7X_LANES = 128
V7X_SUBLANES = 8
V7X_VMEM_LIMIT_BYTES = 56 * 1024 * 1024

SEQ_TILE = 512
SUB_ROWS = 256
TAIL_PIECES = 2
WEIGHT_ROWS = 256
WEIGHT_COLS = 2048
WEIGHT_SLOTS = 4
POOL_HALO = max(POOL_WINDOWS)
POOL_OFF = POOL_HALO + V7X_SUBLANES


def _dot(a, b):
    return jnp.dot(a, b, preferred_element_type=F32)


def _dot_nt(a, b):
    return lax.dot_general(a, b, (((1,), (1,)), ((), ())), preferred_element_type=F32)


def _dot_tn(a, b):
    return lax.dot_general(a, b, (((0,), (0,)), ((), ())), preferred_element_type=F32)


def _rms_norm(x, w):
    return x * lax.rsqrt(jnp.mean(x * x, axis=-1, keepdims=True) + RMS_EPS) * w


def _silu(x):
    return x / (1.0 + jnp.exp(-x))


def _log_sigmoid(z):
    return jnp.minimum(z, 0.0) - jnp.log(1.0 + jnp.exp(-jnp.abs(z)))


def _split_bf16(x):
    hi = x.astype(BF16)
    lo = (x - hi.astype(F32)).astype(BF16)
    return hi, lo


def _pool_layer(x, carry, first_frame, w, uext, s1, s2, s3):
    nw_ref, pin_ref, gw_ref, gb_ref, ps_ref, pout_ref = w
    rows, d_model = x.shape
    gdim = d_model // len(POOL_WINDOWS)
    ext = rows + POOL_HALO
    lo_row = V7X_SUBLANES

    normed = _rms_norm(x, nw_ref[0:1, :]).astype(BF16)
    u = _dot(normed, pin_ref[:, :d_model])
    uext[lo_row:POOL_OFF, :] = carry["u_tail"]
    uext[POOL_OFF:POOL_OFF + rows, :] = u
    carry["u_tail"] = u[rows - POOL_HALO:, :]
    yield
    gate = _dot(normed, pin_ref[:, d_model:])

    a1 = uext[lo_row:lo_row + ext, :] + uext[lo_row - 1:lo_row - 1 + ext, :]
    s1[lo_row:lo_row + ext, :] = a1[:, gdim:]
    a2 = s1[lo_row:lo_row + ext, :] + s1[lo_row - 2:lo_row - 2 + ext, :]
    s2[lo_row:lo_row + ext, :] = a2[:, gdim:]
    a3 = s2[lo_row:lo_row + ext, :] + s2[lo_row - 4:lo_row - 4 + ext, :]
    s3[lo_row:lo_row + ext, :] = a3[:, gdim:]
    a4 = s3[lo_row:lo_row + ext, :] + s3[lo_row - 8:lo_row - 8 + ext, :]
    sums = (a1[POOL_HALO:, :gdim], a2[POOL_HALO:, :gdim], a3[POOL_HALO:, :gdim], a4[POOL_HALO:, :])

    frame = first_frame + lax.broadcasted_iota(jnp.int32, (rows, 1), 0) + 1
    ys = []
    for g, win in enumerate(POOL_WINDOWS):
        cols = slice(g * gdim, (g + 1) * gdim)
        inv_count = 1.0 / jnp.minimum(frame, win).astype(F32)
        pooled = sums[g] * inv_count - u[:, cols]
        mixed = _dot(pooled.astype(BF16), gw_ref[cols, :]) + gb_ref[:, cols]
        ys.append(mixed * ps_ref[:, cols] * _silu(gate[:, cols]))
        yield
    y = jnp.concatenate(ys, axis=-1).astype(BF16)
    yield
    return x + _dot(y, pout_ref[...])


def _gla_layer(h, carry, w, head_k, head_v):
    nw_ref, gint_ref, gkw_ref, gkb_ref, hn_ref, _ = w
    rows, d_model = h.shape
    n_chunks = rows // CHUNK
    key_w = GLA_HEADS * head_k
    heads = range(GLA_HEADS)
    ksl = [slice(hd * head_k, (hd + 1) * head_k) for hd in heads]
    vsl = [slice(hd * head_v, (hd + 1) * head_v) for hd in heads]
    csl = [slice(n * CHUNK, (n + 1) * CHUNK) for n in range(n_chunks)]

    n2 = _rms_norm(h, nw_ref[1:2, :]).astype(BF16)
    qkvg_w = 2 * (key_w + d_model)
    glow = _dot_nt(gint_ref[qkvg_w:, :], n2).T.astype(BF16)
    qk = _dot_nt(n2, gint_ref[:2 * key_w, :])
    z = _dot(glow, gkw_ref[...]) + gkb_ref[...]
    yield

    vg = _dot_nt(n2, gint_ref[2 * key_w:qkvg_w, :])
    lg = _log_sigmoid(z) * (1.0 / GLA_GATE_NORMALIZER)
    lg_hi, lg_lo = _split_bf16(lg)
    r = lax.broadcasted_iota(jnp.int32, (rows, rows), 0)
    c = lax.broadcasted_iota(jnp.int32, (rows, rows), 1)
    same_chunk = (r // CHUNK) == (c // CHUNK)
    causal = c <= r
    tril = jnp.where(same_chunk & causal, 1.0, 0.0).astype(BF16)
    cum = _dot(tril, lg_hi) + _dot(tril, lg_lo)
    yield

    totals = [cum[cs.stop - 1:cs.stop, :] for cs in csl]
    tot = jnp.concatenate([jnp.broadcast_to(t, (CHUNK, key_w)) for t in totals], axis=0)
    e_pos = jnp.exp(cum)
    e_neg = jnp.exp(-cum)
    q = qk[:, :key_w] * (head_k ** -0.5)
    k = qk[:, key_w:]
    q_pos = (q * e_pos).astype(BF16)
    q_neg = (q * e_neg).astype(BF16)
    k_neg = (k * e_neg).astype(BF16)
    k_pos = (k * e_pos).astype(BF16)
    k_dec = (k * jnp.exp(tot - cum)).astype(BF16)
    v = vg[:, :d_model].astype(BF16)
    gate_act = _silu(vg[:, d_model:])
    decay_t = [[jnp.broadcast_to(jnp.exp(t[:, ks]), (V7X_LANES, head_k)).T for t in totals]
               for ks in ksl]
    yield

    fwd = [_dot_nt(q_pos[:, ks], k_neg[:, ks]) for ks in ksl]
    bwd = [_dot_nt(q_neg[:, ks], k_pos[:, ks]) for ks in ksl]
    upd = [[_dot_tn(k_dec[cs, ks], v[cs, vs]) for cs in csl]
           for ks, vs in zip(ksl, vsl)]
    yield

    scores = [jnp.where(same_chunk, jnp.where(causal, fwd[hd], bwd[hd]), 0.0).astype(BF16)
              for hd in heads]
    entering = []
    for hd in heads:
        st = carry["states"][hd]
        ent_h = []
        for n in range(n_chunks):
            ent_h.append(st.astype(BF16))
            d_n = jnp.concatenate([decay_t[hd][n]] * (head_v // V7X_LANES), axis=-1)
            st = st * d_n + upd[hd][n]
        carry["states"][hd] = st
        entering.append(ent_h)
        yield

    outs = []
    for hd in heads:
        inter = [_dot(q_pos[cs, ksl[hd]], entering[hd][n]) for n, cs in enumerate(csl)]
        o_h = _dot(scores[hd], v[:, vsl[hd]]) + jnp.concatenate(inter, axis=0)
        o_h = o_h * lax.rsqrt(jnp.mean(o_h * o_h, axis=-1, keepdims=True) + RMS_EPS) * hn_ref[:, vsl[hd]]
        outs.append(o_h)
        yield
    return (jnp.concatenate(outs, axis=-1) * gate_act).astype(BF16)


def _sub_block(x_ref, o_ref, rows, first_frame, carry, pool_w, gla_w, fnw_ref, pool_scratch,
               head_k, head_v):
    h = yield from _pool_layer(x_ref[0, rows, :], carry, first_frame, pool_w, *pool_scratch)
    yield
    y2 = yield from _gla_layer(h, carry, gla_w, head_k, head_v)
    n_rows = rows.stop - rows.start
    piece = n_rows // TAIL_PIECES
    for p in range(TAIL_PIECES):
        part = slice(p * piece, (p + 1) * piece)
        h2 = h[part, :] + _dot(y2[part, :], gla_w[-1][...])
        o_ref[0, rows.start + p * piece:rows.start + (p + 1) * piece, :] = _rms_norm(h2, fnw_ref[...])


def _load_weights(pairs, stage, sems):
    chunks = []
    for src, dst in pairs:
        n_rows, n_cols = src.shape
        for r0 in range(0, n_rows, WEIGHT_ROWS):
            chunks.append((src, dst, r0, min(WEIGHT_ROWS, n_rows - r0), n_cols))
    ahead = WEIGHT_SLOTS - 1

    def copy(i):
        src, _, r0, n, c = chunks[i]
        slot = i % WEIGHT_SLOTS
        return pltpu.make_async_copy(src.at[r0:r0 + n, :], stage.at[slot, 0:n, 0:c], sems.at[slot])

    for i in range(min(ahead, len(chunks))):
        copy(i).start()
    for i, (_, dst, r0, n, c) in enumerate(chunks):
        if i + ahead < len(chunks):
            copy(i + ahead).start()
        copy(i).wait()
        dst[r0:r0 + n, :] = stage[i % WEIGHT_SLOTS, 0:n, 0:c].astype(BF16)


def _fused_kernel(x_ref, nw_ref, gb_ref, ps_ref, gkw_ref, gkb_ref, hn_ref, fnw_ref,
                  pin_hbm, gw_hbm, pout_hbm, gint_hbm, gout_hbm,
                  o_ref,
                  pin_ref, gw_ref, pout_ref, gint_ref, gout_ref, stage, sems,
                  uext, s1, s2, s3, halo, state,
                  *, ts, d_model, head_k, head_v):
    j = pl.program_id(1)
    n_sub = ts // SUB_ROWS
    lo_row = V7X_SUBLANES

    @pl.when(jnp.logical_and(pl.program_id(0) == 0, j == 0))
    def _():
        _load_weights([(pin_hbm, pin_ref), (gw_hbm, gw_ref), (pout_hbm, pout_ref),
                       (gint_hbm, gint_ref), (gout_hbm, gout_ref)], stage, sems)
        pad_rows = gint_ref.shape[0] - gint_hbm.shape[0]
        gint_ref[gint_hbm.shape[0]:, :] = jnp.zeros((pad_rows, gint_ref.shape[1]), BF16)
        for buf in (uext, s1, s2, s3):
            buf[:, 0:lo_row, :] = jnp.zeros((n_sub, lo_row, buf.shape[2]), F32)

    @pl.when(j == 0)
    def _():
        halo[...] = jnp.zeros_like(halo)
        state[...] = jnp.zeros_like(state)

    pool_w = (nw_ref, pin_ref, gw_ref, gb_ref, ps_ref, pout_ref)
    gla_w = (nw_ref, gint_ref, gkw_ref, gkb_ref, hn_ref, gout_ref)

    carry = {"u_tail": halo[...], "states": [state[hd] for hd in range(GLA_HEADS)]}
    blocks = [
        _sub_block(x_ref, o_ref, slice(s * SUB_ROWS, (s + 1) * SUB_ROWS), j * ts + s * SUB_ROWS,
                   carry, pool_w, gla_w, fnw_ref, (uext.at[s], s1.at[s], s2.at[s], s3.at[s]),
                   head_k, head_v)
        for s in range(n_sub)
    ]
    while blocks:
        for blk in list(blocks):
            try:
                next(blk)
            except StopIteration:
                blocks.remove(blk)
    halo[...] = carry["u_tail"]
    for hd in range(GLA_HEADS):
        state[hd] = carry["states"][hd]


def _resident(shape):
    zeros = (0,) * len(shape)
    return pl.BlockSpec(shape, lambda b, j: zeros, pipeline_mode=pl.Buffered(1))


def kernel(x, norm_w, pool_in_w, pool_group_w, pool_group_b, pool_scale, pool_out_w,
           gla_in_w, gla_gk_w, gla_gk_b, gla_head_norm_w, gla_out_w, final_norm_w):
    batch, seq, d_model = x.shape
    ts = SEQ_TILE
    assert seq % ts == 0 and ts % SUB_ROWS == 0 and SUB_ROWS % CHUNK == 0
    assert norm_w.shape[0] == 2 and pool_in_w.shape[0] == 1 and gla_in_w.shape[0] == 1
    key_w = gla_gk_w.shape[-1]
    head_k = key_w // GLA_HEADS
    head_v = d_model // GLA_HEADS
    n_groups = len(POOL_WINDOWS)
    gdim = d_model // n_groups
    qkvg_w = 2 * key_w + 2 * d_model
    assert gla_in_w.shape[-1] == qkvg_w + GLA_GATE_RANK

    rank_pad = V7X_LANES - GLA_GATE_RANK
    gkw = jnp.pad(gla_gk_w[0], ((0, rank_pad), (0, 0))).astype(BF16)
    small = (norm_w, pool_group_b[0].reshape(1, d_model), pool_scale[0].reshape(1, d_model),
             gkw, gla_gk_b[0].reshape(1, key_w),
             jnp.tile(gla_head_norm_w[0], GLA_HEADS).reshape(1, d_model),
             final_norm_w.reshape(1, d_model))
    big = (pool_in_w[0], pool_group_w[0].reshape(d_model, gdim), pool_out_w[0],
           gla_in_w[0].T, gla_out_w[0])
    assert all(w.shape[1] <= WEIGHT_COLS for w in big)

    operands = (x,) + small + big
    tile = pl.BlockSpec((1, ts, d_model), lambda b, j: (b, j, 0))
    in_specs = ([tile] + [_resident(op.shape) for op in small]
                + [pl.BlockSpec(memory_space=pl.ANY)] * len(big))
    n_sub = ts // SUB_ROWS
    rows = SUB_ROWS + POOL_OFF
    body = functools.partial(_fused_kernel, ts=ts, d_model=d_model, head_k=head_k, head_v=head_v)
    return pl.pallas_call(
        body,
        grid=(batch, seq // ts),
        in_specs=in_specs,
        out_specs=tile,
        out_shape=jax.ShapeDtypeStruct(x.shape, x.dtype),
        scratch_shapes=[
            pltpu.VMEM(pool_in_w.shape[1:], BF16),
            pltpu.VMEM((d_model, gdim), BF16),
            pltpu.VMEM(pool_out_w.shape[1:], BF16),
            pltpu.VMEM((qkvg_w + V7X_LANES, d_model), BF16),
            pltpu.VMEM(gla_out_w.shape[1:], BF16),
            pltpu.VMEM((WEIGHT_SLOTS, WEIGHT_ROWS, WEIGHT_COLS), F32),
            pltpu.SemaphoreType.DMA((WEIGHT_SLOTS,)),
            pltpu.VMEM((n_sub, rows, d_model), F32),
            pltpu.VMEM((n_sub, rows, d_model - gdim), F32),
            pltpu.VMEM((n_sub, rows, d_model - 2 * gdim), F32),
            pltpu.VMEM((n_sub, rows, d_model - 3 * gdim), F32),
            pltpu.VMEM((POOL_HALO, d_model), F32),
            pltpu.VMEM((GLA_HEADS, head_k, head_v), F32),
        ],
        compiler_params=pltpu.CompilerParams(
            dimension_semantics=("arbitrary", "arbitrary"),
            vmem_limit_bytes=V7X_VMEM_LIMIT_BYTES,
        ),
        name="pool_gla_trunk",
    )(*operands)
```

```python
import functools

import jax
import jax.numpy as jnp
from jax import lax
from jax.experimental import pallas as pl
from jax.experimental.pallas import tpu as pltpu

F32 = jnp.float32
BF16 = jnp.bfloat16

RMS_EPS = 1e-6
CHUNK = 64
POOL_WINDOWS = (2, 4, 8, 16)
GLA_HEADS = 4
GLA_GATE_NORMALIZER = 16.0
GLA_GATE_RANK = 16

V7X_LANES = 128
V7X_SUBLANES = 8
V7X_VMEM_LIMIT_BYTES = 56 * 1024 * 1024

SEQ_TILE = 512
SUB_ROWS = 256
TAIL_PIECES = 2
WEIGHT_ROWS = 256
WEIGHT_COLS = 2048
WEIGHT_SLOTS = 4
POOL_PHASES_PER_ROUND = (1, 0, 1, 0, 2, 2, 2, 2, 1, 1, 1, 1)
POOL_HALO = max(POOL_WINDOWS)
POOL_OFF = POOL_HALO + V7X_SUBLANES


def _dot(a, b):
    return jnp.dot(a, b, preferred_element_type=F32)


def _dot_nt(a, b):
    return lax.dot_general(a, b, (((1,), (1,)), ((), ())), preferred_element_type=F32)


def _dot_tn(a, b):
    return lax.dot_general(a, b, (((0,), (0,)), ((), ())), preferred_element_type=F32)


def _rms_norm(x, w):
    return x * lax.rsqrt(jnp.mean(x * x, axis=-1, keepdims=True) + RMS_EPS) * w


def _silu(x):
    return x / (1.0 + jnp.exp(-x))


def _log_sigmoid(z):
    return jnp.minimum(z, 0.0) - jnp.log(1.0 + jnp.exp(-jnp.abs(z)))


def _split_bf16(x):
    hi = x.astype(BF16)
    lo = (x - hi.astype(F32)).astype(BF16)
    return hi, lo


def _pool_layer(x, carry, first_frame, w, uext, s1, s2, s3):
    nw_ref, pin_ref, gw_ref, gb_ref, ps_ref, pout_ref = w
    rows, d_model = x.shape
    gdim = d_model // len(POOL_WINDOWS)
    ext = rows + POOL_HALO
    lo_row = V7X_SUBLANES

    normed = _rms_norm(x, nw_ref[0:1, :]).astype(BF16)
    u = _dot(normed, pin_ref[:, :d_model])
    uext[lo_row:POOL_OFF, :] = carry["u_tail"]
    uext[POOL_OFF:POOL_OFF + rows, :] = u
    carry["u_tail"] = u[rows - POOL_HALO:, :]
    yield
    gate = _dot(normed, pin_ref[:, d_model:])

    a1 = uext[lo_row:lo_row + ext, :] + uext[lo_row - 1:lo_row - 1 + ext, :]
    s1[lo_row:lo_row + ext, :] = a1[:, gdim:]
    a2 = s1[lo_row:lo_row + ext, :] + s1[lo_row - 2:lo_row - 2 + ext, :]
    s2[lo_row:lo_row + ext, :] = a2[:, gdim:]
    a3 = s2[lo_row:lo_row + ext, :] + s2[lo_row - 4:lo_row - 4 + ext, :]
    s3[lo_row:lo_row + ext, :] = a3[:, gdim:]
    a4 = s3[lo_row:lo_row + ext, :] + s3[lo_row - 8:lo_row - 8 + ext, :]
    sums = (a1[POOL_HALO:, :gdim], a2[POOL_HALO:, :gdim], a3[POOL_HALO:, :gdim], a4[POOL_HALO:, :])

    frame = first_frame + lax.broadcasted_iota(jnp.int32, (rows, 1), 0) + 1
    ys = []
    for g, win in enumerate(POOL_WINDOWS):
        cols = slice(g * gdim, (g + 1) * gdim)
        inv_count = 1.0 / jnp.minimum(frame, win).astype(F32)
        pooled = sums[g] * inv_count - u[:, cols]
        mixed = _dot(pooled.astype(BF16), gw_ref[cols, :]) + gb_ref[:, cols]
        ys.append(mixed * ps_ref[:, cols] * _silu(gate[:, cols]))
        yield
    y = jnp.concatenate(ys, axis=-1).astype(BF16)
    yield
    return x + _dot(y, pout_ref[...])


def _gla_layer(h, carry, w, head_k, head_v):
    nw_ref, gint_ref, gkw_ref, gkb_ref, hn_ref, _ = w
    rows, d_model = h.shape
    n_chunks = rows // CHUNK
    key_w = GLA_HEADS * head_k
    heads = range(GLA_HEADS)
    ksl = [slice(hd * head_k, (hd + 1) * head_k) for hd in heads]
    vsl = [slice(hd * head_v, (hd + 1) * head_v) for hd in heads]
    csl = [slice(n * CHUNK, (n + 1) * CHUNK) for n in range(n_chunks)]

    n2 = _rms_norm(h, nw_ref[1:2, :]).astype(BF16)
    qkvg_w = 2 * (key_w + d_model)
    glow_t = _dot_nt(gint_ref[qkvg_w:, :], n2).astype(BF16)
    qk = _dot_nt(n2, gint_ref[:2 * key_w, :])
    z = _dot_tn(glow_t, gkw_ref[...]) + gkb_ref[...]
    yield

    vg = _dot_nt(n2, gint_ref[2 * key_w:qkvg_w, :])
    lg = _log_sigmoid(z) * (1.0 / GLA_GATE_NORMALIZER)
    lg_hi, lg_lo = _split_bf16(lg)
    r = lax.broadcasted_iota(jnp.int32, (rows, rows), 0)
    c = lax.broadcasted_iota(jnp.int32, (rows, rows), 1)
    same_chunk = (r // CHUNK) == (c // CHUNK)
    causal = c <= r
    tril = jnp.where(same_chunk & causal, 1.0, 0.0).astype(BF16)
    cum = _dot(tril, lg_hi) + _dot(tril, lg_lo)
    yield

    totals = [cum[cs.stop - 1:cs.stop, :] for cs in csl]
    tot = jnp.concatenate([jnp.broadcast_to(t, (CHUNK, key_w)) for t in totals], axis=0)
    e_pos = jnp.exp(cum)
    e_neg = jnp.exp(-cum)
    q = qk[:, :key_w] * (head_k ** -0.5)
    k = qk[:, key_w:]
    q_pos = (q * e_pos).astype(BF16)
    q_neg = (q * e_neg).astype(BF16)
    k_neg = (k * e_neg).astype(BF16)
    k_pos = (k * e_pos).astype(BF16)
    k_dec = (k * jnp.exp(tot - cum)).astype(BF16)
    v = vg[:, :d_model].astype(BF16)
    gate_act = _silu(vg[:, d_model:])
    decay_t = [[jnp.broadcast_to(jnp.exp(t[:, ks]), (V7X_LANES, head_k)).T for t in totals]
               for ks in ksl]
    yield

    fwd = [_dot_nt(q_pos[:, ks], k_neg[:, ks]) for ks in ksl]
    bwd = [_dot_nt(q_neg[:, ks], k_pos[:, ks]) for ks in ksl]
    upd = [[_dot_tn(k_dec[cs, ks], v[cs, vs]) for cs in csl]
           for ks, vs in zip(ksl, vsl)]
    yield

    scores = [jnp.where(same_chunk, jnp.where(causal, fwd[hd], bwd[hd]), 0.0).astype(BF16)
              for hd in heads]
    entering = []
    for hd in heads:
        st = carry["states"][hd]
        ent_h = []
        for n in range(n_chunks):
            ent_h.append(st.astype(BF16))
            d_n = jnp.concatenate([decay_t[hd][n]] * (head_v // V7X_LANES), axis=-1)
            st = st * d_n + upd[hd][n]
        carry["states"][hd] = st
        entering.append(ent_h)
        yield

    outs = []
    for hd in heads:
        inter = [_dot(q_pos[cs, ksl[hd]], entering[hd][n]) for n, cs in enumerate(csl)]
        o_h = _dot(scores[hd], v[:, vsl[hd]]) + jnp.concatenate(inter, axis=0)
        o_h = o_h * lax.rsqrt(jnp.mean(o_h * o_h, axis=-1, keepdims=True) + RMS_EPS) * hn_ref[:, vsl[hd]]
        outs.append(o_h)
        yield
    return (jnp.concatenate(outs, axis=-1) * gate_act).astype(BF16)


def _pool_stream(x_ref, rows, first_frame, carry, pool_w, pool_scratch, h_out):
    h = yield from _pool_layer(x_ref[0, rows, :], carry, first_frame, pool_w, *pool_scratch)
    h_out[...] = h


def _gla_stream(h_in, o_ref, rows, carry, gla_w, fnw_ref, head_k, head_v):
    h = h_in[...]
    y2 = yield from _gla_layer(h, carry, gla_w, head_k, head_v)
    n_rows = rows.stop - rows.start
    piece = n_rows // TAIL_PIECES
    for p in range(TAIL_PIECES):
        part = slice(p * piece, (p + 1) * piece)
        h2 = h[part, :] + _dot(y2[part, :], gla_w[-1][...])
        o_ref[0, rows.start + p * piece:rows.start + (p + 1) * piece, :] = _rms_norm(h2, fnw_ref[...])
        yield


def _advance(stream):
    try:
        next(stream)
        return True
    except StopIteration:
        return False


def _load_weights(pairs, stage, sems):
    chunks = []
    for src, dst in pairs:
        n_rows, n_cols = src.shape
        for r0 in range(0, n_rows, WEIGHT_ROWS):
            chunks.append((src, dst, r0, min(WEIGHT_ROWS, n_rows - r0), n_cols))
    ahead = WEIGHT_SLOTS - 1

    def copy(i):
        src, _, r0, n, c = chunks[i]
        slot = i % WEIGHT_SLOTS
        return pltpu.make_async_copy(src.at[r0:r0 + n, :], stage.at[slot, 0:n, 0:c], sems.at[slot])

    for i in range(min(ahead, len(chunks))):
        copy(i).start()
    for i, (_, dst, r0, n, c) in enumerate(chunks):
        if i + ahead < len(chunks):
            copy(i + ahead).start()
        copy(i).wait()
        dst[r0:r0 + n, :] = stage[i % WEIGHT_SLOTS, 0:n, 0:c].astype(BF16)


def _fused_kernel(x0_ref, xn_ref, nw_ref, gb_ref, ps_ref, gkw_ref, gkb_ref, hn_ref, fnw_ref,
                  pin_hbm, gw_hbm, pout_hbm, gint_hbm, gout_hbm,
                  o_ref,
                  pin_ref, gw_ref, pout_ref, gint_ref, gout_ref, stage, sems,
                  uext, s1, s2, s3, halo, state, hbuf,
                  *, ts, n_tiles, tiles_per_seq, d_model, head_k, head_v):
    t = pl.program_id(0)
    n_sub = ts // SUB_ROWS
    lo_row = V7X_SUBLANES
    sub_rows = [slice(s * SUB_ROWS, (s + 1) * SUB_ROWS) for s in range(n_sub)]
    pool_w = (nw_ref, pin_ref, gw_ref, gb_ref, ps_ref, pout_ref)
    gla_w = (nw_ref, gint_ref, gkw_ref, gkb_ref, hn_ref, gout_ref)
    pool_scratch = [(uext.at[s], s1.at[s], s2.at[s], s3.at[s]) for s in range(n_sub)]

    @pl.when(t == 0)
    def _():
        _load_weights([(pin_hbm, pin_ref), (gw_hbm, gw_ref), (pout_hbm, pout_ref),
                       (gint_hbm, gint_ref), (gout_hbm, gout_ref)], stage, sems)
        for buf in (uext, s1, s2, s3):
            buf[:, 0:lo_row, :] = jnp.zeros((n_sub, lo_row, buf.shape[2]), F32)
        first = {"u_tail": jnp.zeros(halo.shape, F32)}
        for s in range(n_sub):
            for _ in _pool_stream(x0_ref, sub_rows[s], s * SUB_ROWS, first, pool_w,
                                  pool_scratch[s], hbuf.at[s]):
                pass
        halo[...] = first["u_tail"]

    @pl.when(t % tiles_per_seq == 0)
    def _():
        state[...] = jnp.zeros_like(state)

    j_next = jnp.minimum(t + 1, n_tiles - 1) % tiles_per_seq
    starts_seq = j_next == 0
    carry = {"u_tail": jnp.where(starts_seq, 0.0, halo[...]),
             "states": [state[hd] for hd in range(GLA_HEADS)]}
    gla = [_gla_stream(hbuf.at[s], o_ref, sub_rows[s], carry, gla_w, fnw_ref, head_k, head_v)
           for s in range(n_sub)]
    pool = [_pool_stream(xn_ref, sub_rows[s], j_next * ts + s * SUB_ROWS, carry, pool_w,
                         pool_scratch[s], hbuf.at[s])
            for s in range(n_sub)]
    rnd = 0
    while gla or pool:
        gla = [g for g in gla if _advance(g)]
        n_pool = POOL_PHASES_PER_ROUND[rnd] if rnd < len(POOL_PHASES_PER_ROUND) else 1
        for _ in range(n_pool):
            if pool and not _advance(pool[0]):
                pool.pop(0)
        rnd += 1
    halo[...] = carry["u_tail"]
    for hd in range(GLA_HEADS):
        state[hd] = carry["states"][hd]


def _resident(shape):
    zeros = (0,) * len(shape)
    return pl.BlockSpec(shape, lambda t: zeros, pipeline_mode=pl.Buffered(1))


def kernel(x, norm_w, pool_in_w, pool_group_w, pool_group_b, pool_scale, pool_out_w,
           gla_in_w, gla_gk_w, gla_gk_b, gla_head_norm_w, gla_out_w, final_norm_w):
    batch, seq, d_model = x.shape
    ts = SEQ_TILE
    assert seq % ts == 0 and ts % SUB_ROWS == 0 and SUB_ROWS % CHUNK == 0
    assert norm_w.shape[0] == 2 and pool_in_w.shape[0] == 1 and gla_in_w.shape[0] == 1
    key_w = gla_gk_w.shape[-1]
    head_k = key_w // GLA_HEADS
    head_v = d_model // GLA_HEADS
    n_groups = len(POOL_WINDOWS)
    gdim = d_model // n_groups
    qkvg_w = 2 * key_w + 2 * d_model
    assert gla_in_w.shape[-1] == qkvg_w + GLA_GATE_RANK
    tiles_per_seq = seq // ts
    n_tiles = batch * tiles_per_seq

    small = (norm_w, pool_group_b[0].reshape(1, d_model), pool_scale[0].reshape(1, d_model),
             gla_gk_w[0].astype(BF16), gla_gk_b[0].reshape(1, key_w),
             jnp.tile(gla_head_norm_w[0], GLA_HEADS).reshape(1, d_model),
             final_norm_w.reshape(1, d_model))
    big = (pool_in_w[0], pool_group_w[0].reshape(d_model, gdim), pool_out_w[0],
           gla_in_w[0].T, gla_out_w[0])
    assert all(w.shape[1] <= WEIGHT_COLS for w in big)

    def tile_of(i):
        return (i // tiles_per_seq, i % tiles_per_seq, 0)

    block = (1, ts, d_model)
    first_tile = pl.BlockSpec(block, lambda t: (0, 0, 0), pipeline_mode=pl.Buffered(1))
    next_tile = pl.BlockSpec(block, lambda t: tile_of(jnp.minimum(t + 1, n_tiles - 1)))
    this_tile = pl.BlockSpec(block, lambda t: tile_of(t))
    operands = (x, x) + small + big
    in_specs = ([first_tile, next_tile] + [_resident(op.shape) for op in small]
                + [pl.BlockSpec(memory_space=pl.ANY)] * len(big))
    n_sub = ts // SUB_ROWS
    rows = SUB_ROWS + POOL_OFF
    body = functools.partial(_fused_kernel, ts=ts, n_tiles=n_tiles, tiles_per_seq=tiles_per_seq,
                             d_model=d_model, head_k=head_k, head_v=head_v)
    return pl.pallas_call(
        body,
        grid=(n_tiles,),
        in_specs=in_specs,
        out_specs=this_tile,
        out_shape=jax.ShapeDtypeStruct(x.shape, x.dtype),
        scratch_shapes=[
            pltpu.VMEM(pool_in_w.shape[1:], BF16),
            pltpu.VMEM((d_model, gdim), BF16),
            pltpu.VMEM(pool_out_w.shape[1:], BF16),
            pltpu.VMEM((gla_in_w.shape[2], d_model), BF16),
            pltpu.VMEM(gla_out_w.shape[1:], BF16),
            pltpu.VMEM((WEIGHT_SLOTS, WEIGHT_ROWS, WEIGHT_COLS), F32),
            pltpu.SemaphoreType.DMA((WEIGHT_SLOTS,)),
            pltpu.VMEM((n_sub, rows, d_model), F32),
            pltpu.VMEM((n_sub, rows, d_model - gdim), F32),
            pltpu.VMEM((n_sub, rows, d_model - 2 * gdim), F32),
            pltpu.VMEM((n_sub, rows, d_model - 3 * gdim), F32),
            pltpu.VMEM((POOL_HALO, d_model), F32),
            pltpu.VMEM((GLA_HEADS, head_k, head_v), F32),
            pltpu.VMEM((n_sub, SUB_ROWS, d_model), F32),
        ],
        compiler_params=pltpu.CompilerParams(
            dimension_semantics=("arbitrary",),
            vmem_limit_bytes=V7X_VMEM_LIMIT_BYTES,
        ),
        name="pool_gla_trunk",
    )(*operands)
```

```python
import functools

import jax
import jax.numpy as jnp
from jax import lax
from jax.experimental import pallas as pl
from jax.experimental.pallas import tpu as pltpu

F32 = jnp.float32
BF16 = jnp.bfloat16

RMS_EPS = 1e-6
CHUNK = 64
POOL_WINDOWS = (2, 4, 8, 16)
GLA_HEADS = 4
GLA_GATE_NORMALIZER = 16.0
GLA_GATE_RANK = 16

V7X_LANES = 128
V7X_SUBLANES = 8
V7X_VMEM_LIMIT_BYTES = 56 * 1024 * 1024

SEQ_TILE = 512
SUB_ROWS = 256
TAIL_PIECES = 2
WEIGHT_ROWS = 512
WEIGHT_COLS = 2048
WEIGHT_SLOTS = 3
POOL_HALO = max(POOL_WINDOWS)
POOL_OFF = POOL_HALO + V7X_SUBLANES


def _dot(a, b):
    return jnp.dot(a, b, preferred_element_type=F32)


def _dot_nt(a, b):
    return lax.dot_general(a, b, (((1,), (1,)), ((), ())), preferred_element_type=F32)


def _dot_tn(a, b):
    return lax.dot_general(a, b, (((0,), (0,)), ((), ())), preferred_element_type=F32)


def _rms_norm(x, w):
    return x * lax.rsqrt(jnp.mean(x * x, axis=-1, keepdims=True) + RMS_EPS) * w


def _silu(x):
    return x / (1.0 + jnp.exp(-x))


def _log_sigmoid(z):
    return jnp.minimum(z, 0.0) - jnp.log(1.0 + jnp.exp(-jnp.abs(z)))


def _split_bf16(x):
    hi = x.astype(BF16)
    lo = (x - hi.astype(F32)).astype(BF16)
    return hi, lo


def _pool_layer(x, carry, first_frame, w, uext, s1, s2, s3):
    nw_ref, pin_ref, gw_ref, gb_ref, ps_ref, pout_ref = w
    rows, d_model = x.shape
    gdim = d_model // len(POOL_WINDOWS)
    ext = rows + POOL_HALO
    lo_row = V7X_SUBLANES

    normed = _rms_norm(x, nw_ref[0:1, :]).astype(BF16)
    u = _dot(normed, pin_ref[:, :d_model])
    uext[lo_row:POOL_OFF, :] = carry["u_tail"]
    uext[POOL_OFF:POOL_OFF + rows, :] = u
    carry["u_tail"] = u[rows - POOL_HALO:, :]
    yield
    gate = _dot(normed, pin_ref[:, d_model:])

    a1 = uext[lo_row:lo_row + ext, :] + uext[lo_row - 1:lo_row - 1 + ext, :]
    s1[lo_row:lo_row + ext, :] = a1[:, gdim:]
    a2 = s1[lo_row:lo_row + ext, :] + s1[lo_row - 2:lo_row - 2 + ext, :]
    s2[lo_row:lo_row + ext, :] = a2[:, gdim:]
    a3 = s2[lo_row:lo_row + ext, :] + s2[lo_row - 4:lo_row - 4 + ext, :]
    s3[lo_row:lo_row + ext, :] = a3[:, gdim:]
    a4 = s3[lo_row:lo_row + ext, :] + s3[lo_row - 8:lo_row - 8 + ext, :]
    sums = (a1[POOL_HALO:, :gdim], a2[POOL_HALO:, :gdim], a3[POOL_HALO:, :gdim], a4[POOL_HALO:, :])

    frame = first_frame + lax.broadcasted_iota(jnp.int32, (rows, 1), 0) + 1
    ys = []
    for g, win in enumerate(POOL_WINDOWS):
        cols = slice(g * gdim, (g + 1) * gdim)
        inv_count = 1.0 / jnp.minimum(frame, win).astype(F32)
        pooled = sums[g] * inv_count - u[:, cols]
        mixed = _dot(pooled.astype(BF16), gw_ref[cols, :]) + gb_ref[:, cols]
        ys.append(mixed * ps_ref[:, cols] * _silu(gate[:, cols]))
        yield
    y = jnp.concatenate(ys, axis=-1).astype(BF16)
    yield
    return x + _dot(y, pout_ref[...])


def _gla_layer(h, carry, w, head_k, head_v):
    nw_ref, gint_ref, gkw_ref, gkb_ref, hn_ref, _ = w
    rows, d_model = h.shape
    n_chunks = rows // CHUNK
    key_w = GLA_HEADS * head_k
    heads = range(GLA_HEADS)
    ksl = [slice(hd * head_k, (hd + 1) * head_k) for hd in heads]
    vsl = [slice(hd * head_v, (hd + 1) * head_v) for hd in heads]
    csl = [slice(n * CHUNK, (n + 1) * CHUNK) for n in range(n_chunks)]

    n2 = _rms_norm(h, nw_ref[1:2, :]).astype(BF16)
    qkvg_w = 2 * (key_w + d_model)
    glow_t = _dot_nt(gint_ref[qkvg_w:, :], n2).astype(BF16)
    qk = _dot_nt(n2, gint_ref[:2 * key_w, :])
    z = _dot_tn(glow_t, gkw_ref[...]) + gkb_ref[...]
    yield

    vg = _dot_nt(n2, gint_ref[2 * key_w:qkvg_w, :])
    lg = _log_sigmoid(z) * (1.0 / GLA_GATE_NORMALIZER)
    lg_hi, lg_lo = _split_bf16(lg)
    r = lax.broadcasted_iota(jnp.int32, (rows, rows), 0)
    c = lax.broadcasted_iota(jnp.int32, (rows, rows), 1)
    same_chunk = (r // CHUNK) == (c // CHUNK)
    causal = c <= r
    tril = jnp.where(same_chunk & causal, 1.0, 0.0).astype(BF16)
    cum = _dot(tril, lg_hi) + _dot(tril, lg_lo)
    yield

    totals = [cum[cs.stop - 1:cs.stop, :] for cs in csl]
    tot = jnp.concatenate([jnp.broadcast_to(t, (CHUNK, key_w)) for t in totals], axis=0)
    e_pos = jnp.exp(cum)
    e_neg = jnp.exp(-cum)
    q = qk[:, :key_w] * (head_k ** -0.5)
    k = qk[:, key_w:]
    q_pos = (q * e_pos).astype(BF16)
    q_neg = (q * e_neg).astype(BF16)
    k_neg = (k * e_neg).astype(BF16)
    k_pos = (k * e_pos).astype(BF16)
    k_dec = (k * jnp.exp(tot - cum)).astype(BF16)
    v = vg[:, :d_model].astype(BF16)
    gate_act = _silu(vg[:, d_model:])
    decay_t = [[jnp.broadcast_to(jnp.exp(t[:, ks]), (V7X_LANES, head_k)).T for t in totals]
               for ks in ksl]
    yield

    fwd, bwd, upd, scores, entering, outs = {}, {}, {}, {}, {}, {}

    def state_free_matmuls(hd):
        ks, vs = ksl[hd], vsl[hd]
        fwd[hd] = _dot_nt(q_pos[:, ks], k_neg[:, ks])
        bwd[hd] = _dot_nt(q_neg[:, ks], k_pos[:, ks])
        upd[hd] = [_dot_tn(k_dec[cs, ks], v[cs, vs]) for cs in csl]

    def vector_work(hd):
        scores[hd] = jnp.where(same_chunk, jnp.where(causal, fwd[hd], bwd[hd]), 0.0).astype(BF16)
        st = carry["states"][hd]
        entering[hd] = []
        for n in range(n_chunks):
            entering[hd].append(st.astype(BF16))
            d_n = jnp.concatenate([decay_t[hd][n]] * (head_v // V7X_LANES), axis=-1)
            st = st * d_n + upd[hd][n]
        carry["states"][hd] = st

    def state_matmuls(hd):
        inter = [_dot(q_pos[cs, ksl[hd]], entering[hd][n]) for n, cs in enumerate(csl)]
        o_h = _dot(scores[hd], v[:, vsl[hd]]) + jnp.concatenate(inter, axis=0)
        outs[hd] = o_h * lax.rsqrt(jnp.mean(o_h * o_h, axis=-1, keepdims=True) + RMS_EPS) * hn_ref[:, vsl[hd]]

    for hd in heads:
        state_free_matmuls(hd)
    yield
    for stage in (vector_work, state_matmuls):
        for hd in heads:
            stage(hd)
            yield
    return (jnp.concatenate([outs[hd] for hd in heads], axis=-1) * gate_act).astype(BF16)


def _sub_block(x_ref, o_ref, rows, first_frame, carry, pool_w, gla_w, fnw_ref, pool_scratch,
               head_k, head_v):
    h = yield from _pool_layer(x_ref[0, rows, :], carry, first_frame, pool_w, *pool_scratch)
    yield
    y2 = yield from _gla_layer(h, carry, gla_w, head_k, head_v)
    n_rows = rows.stop - rows.start
    piece = n_rows // TAIL_PIECES
    for p in range(TAIL_PIECES):
        part = slice(p * piece, (p + 1) * piece)
        h2 = h[part, :] + _dot(y2[part, :], gla_w[-1][...])
        o_ref[0, rows.start + p * piece:rows.start + (p + 1) * piece, :] = _rms_norm(h2, fnw_ref[...])


def _load_weights(pairs, stage, sems):
    chunks = []
    for src, dst in pairs:
        n_rows, n_cols = src.shape
        for r0 in range(0, n_rows, WEIGHT_ROWS):
            chunks.append((src, dst, r0, min(WEIGHT_ROWS, n_rows - r0), n_cols))
    ahead = WEIGHT_SLOTS - 1

    def copy(i):
        src, _, r0, n, c = chunks[i]
        slot = i % WEIGHT_SLOTS
        return pltpu.make_async_copy(src.at[r0:r0 + n, :], stage.at[slot, 0:n, 0:c], sems.at[slot])

    for i in range(min(ahead, len(chunks))):
        copy(i).start()
    for i, (_, dst, r0, n, c) in enumerate(chunks):
        if i + ahead < len(chunks):
            copy(i + ahead).start()
        copy(i).wait()
        dst[r0:r0 + n, :] = stage[i % WEIGHT_SLOTS, 0:n, 0:c].astype(BF16)


def _fused_kernel(x_ref, nw_ref, gb_ref, ps_ref, gkw_ref, gkb_ref, hn_ref, fnw_ref,
                  pin_hbm, gw_hbm, pout_hbm, gint_hbm, gout_hbm,
                  o_ref,
                  pin_ref, gw_ref, pout_ref, gint_ref, gout_ref, stage, sems,
                  uext, s1, s2, s3, halo, state,
                  *, ts, d_model, head_k, head_v):
    j = pl.program_id(1)
    n_sub = ts // SUB_ROWS
    lo_row = V7X_SUBLANES

    @pl.when(jnp.logical_and(pl.program_id(0) == 0, j == 0))
    def _():
        _load_weights([(pin_hbm, pin_ref), (gw_hbm, gw_ref), (pout_hbm, pout_ref),
                       (gint_hbm, gint_ref), (gout_hbm, gout_ref)], stage, sems)
        for buf in (uext, s1, s2, s3):
            buf[:, 0:lo_row, :] = jnp.zeros((n_sub, lo_row, buf.shape[2]), F32)

    @pl.when(j == 0)
    def _():
        halo[...] = jnp.zeros_like(halo)
        state[...] = jnp.zeros_like(state)

    pool_w = (nw_ref, pin_ref, gw_ref, gb_ref, ps_ref, pout_ref)
    gla_w = (nw_ref, gint_ref, gkw_ref, gkb_ref, hn_ref, gout_ref)

    carry = {"u_tail": halo[...], "states": [state[hd] for hd in range(GLA_HEADS)]}
    blocks = [
        _sub_block(x_ref, o_ref, slice(s * SUB_ROWS, (s + 1) * SUB_ROWS), j * ts + s * SUB_ROWS,
                   carry, pool_w, gla_w, fnw_ref, (uext.at[s], s1.at[s], s2.at[s], s3.at[s]),
                   head_k, head_v)
        for s in range(n_sub)
    ]
    while blocks:
        for blk in list(blocks):
            try:
                next(blk)
            except StopIteration:
                blocks.remove(blk)
    halo[...] = carry["u_tail"]
    for hd in range(GLA_HEADS):
        state[hd] = carry["states"][hd]


def _resident(shape):
    zeros = (0,) * len(shape)
    return pl.BlockSpec(shape, lambda b, j: zeros, pipeline_mode=pl.Buffered(1))


def kernel(x, norm_w, pool_in_w, pool_group_w, pool_group_b, pool_scale, pool_out_w,
           gla_in_w, gla_gk_w, gla_gk_b, gla_head_norm_w, gla_out_w, final_norm_w):
    batch, seq, d_model = x.shape
    ts = SEQ_TILE
    assert seq % ts == 0 and ts % SUB_ROWS == 0 and SUB_ROWS % CHUNK == 0
    assert norm_w.shape[0] == 2 and pool_in_w.shape[0] == 1 and gla_in_w.shape[0] == 1
    key_w = gla_gk_w.shape[-1]
    head_k = key_w // GLA_HEADS
    head_v = d_model // GLA_HEADS
    n_groups = len(POOL_WINDOWS)
    gdim = d_model // n_groups
    qkvg_w = 2 * key_w + 2 * d_model
    assert gla_in_w.shape[-1] == qkvg_w + GLA_GATE_RANK

    small = (norm_w, pool_group_b[0].reshape(1, d_model), pool_scale[0].reshape(1, d_model),
             gla_gk_w[0].astype(BF16), gla_gk_b[0].reshape(1, key_w),
             jnp.tile(gla_head_norm_w[0], GLA_HEADS).reshape(1, d_model),
             final_norm_w.reshape(1, d_model))
    big = (pool_in_w[0], pool_group_w[0].reshape(d_model, gdim), pool_out_w[0],
           gla_in_w[0].T, gla_out_w[0])
    assert all(w.shape[1] <= WEIGHT_COLS for w in big)

    operands = (x,) + small + big
    tile = pl.BlockSpec((1, ts, d_model), lambda b, j: (b, j, 0))
    in_specs = ([tile] + [_resident(op.shape) for op in small]
                + [pl.BlockSpec(memory_space=pl.ANY)] * len(big))
    n_sub = ts // SUB_ROWS
    rows = SUB_ROWS + POOL_OFF
    body = functools.partial(_fused_kernel, ts=ts, d_model=d_model, head_k=head_k, head_v=head_v)
    return pl.pallas_call(
        body,
        grid=(batch, seq // ts),
        in_specs=in_specs,
        out_specs=tile,
        out_shape=jax.ShapeDtypeStruct(x.shape, x.dtype),
        scratch_shapes=[
            pltpu.VMEM(pool_in_w.shape[1:], BF16),
            pltpu.VMEM((d_model, gdim), BF16),
            pltpu.VMEM(pool_out_w.shape[1:], BF16),
            pltpu.VMEM((gla_in_w.shape[2], d_model), BF16),
            pltpu.VMEM(gla_out_w.shape[1:], BF16),
            pltpu.VMEM((WEIGHT_SLOTS, WEIGHT_ROWS, WEIGHT_COLS), F32),
            pltpu.SemaphoreType.DMA((WEIGHT_SLOTS,)),
            pltpu.VMEM((n_sub, rows, d_model), F32),
            pltpu.VMEM((n_sub, rows, d_model - gdim), F32),
            pltpu.VMEM((n_sub, rows, d_model - 2 * gdim), F32),
            pltpu.VMEM((n_sub, rows, d_model - 3 * gdim), F32),
            pltpu.VMEM((POOL_HALO, d_model), F32),
            pltpu.VMEM((GLA_HEADS, head_k, head_v), F32),
        ],
        compiler_params=pltpu.CompilerParams(
            dimension_semantics=("arbitrary", "arbitrary"),
            vmem_limit_bytes=V7X_VMEM_LIMIT_BYTES,
        ),
        name="pool_gla_trunk",
    )(*operands)
```

```python
import functools

import jax
import jax.numpy as jnp
from jax import lax
from jax.experimental import pallas as pl
from jax.experimental.pallas import tpu as pltpu

F32 = jnp.float32
BF16 = jnp.bfloat16

RMS_EPS = 1e-6
CHUNK = 64
POOL_WINDOWS = (2, 4, 8, 16)
GLA_HEADS = 4
GLA_GATE_NORMALIZER = 16.0
GLA_GATE_RANK = 16

V7X_LANES = 128
V7X_SUBLANES = 8
V7X_VMEM_LIMIT_BYTES = 56 * 1024 * 1024

SEQ_TILE = 512
SUB_ROWS = 256
TAIL_PIECES = 2
WEIGHT_ROWS = 512
WEIGHT_COLS = 2048
WEIGHT_SLOTS = 3
POOL_HALO = max(POOL_WINDOWS)
POOL_OFF = POOL_HALO + V7X_SUBLANES


def _dot(a, b):
    return jnp.dot(a, b, preferred_element_type=F32)


def _dot_nt(a, b):
    return lax.dot_general(a, b, (((1,), (1,)), ((), ())), preferred_element_type=F32)


def _dot_tn(a, b):
    return lax.dot_general(a, b, (((0,), (0,)), ((), ())), preferred_element_type=F32)


def _rms_norm(x, w):
    return x * lax.rsqrt(jnp.mean(x * x, axis=-1, keepdims=True) + RMS_EPS) * w


def _silu(x):
    return x / (1.0 + jnp.exp(-x))


def _log_sigmoid(z):
    return jnp.minimum(z, 0.0) - jnp.log(1.0 + jnp.exp(-jnp.abs(z)))


def _split_bf16(x):
    hi = x.astype(BF16)
    lo = (x - hi.astype(F32)).astype(BF16)
    return hi, lo


def _pool_layer(x, carry, first_frame, w, uext, s1, s2, s3):
    nw_ref, pin_ref, gw_ref, gb_ref, ps_ref, pout_ref = w
    rows, d_model = x.shape
    gdim = d_model // len(POOL_WINDOWS)
    ext = rows + POOL_HALO
    lo_row = V7X_SUBLANES

    normed = _rms_norm(x, nw_ref[0:1, :]).astype(BF16)
    u = _dot(normed, pin_ref[:, :d_model])
    uext[lo_row:POOL_OFF, :] = carry["u_tail"]
    uext[POOL_OFF:POOL_OFF + rows, :] = u
    carry["u_tail"] = u[rows - POOL_HALO:, :]
    yield
    a1 = uext[lo_row:lo_row + ext, :] + uext[lo_row - 1:lo_row - 1 + ext, :]
    s1[lo_row:lo_row + ext, :] = a1[:, gdim:]
    a2 = s1[lo_row:lo_row + ext, :] + s1[lo_row - 2:lo_row - 2 + ext, :]
    s2[lo_row:lo_row + ext, :] = a2[:, gdim:]
    a3 = s2[lo_row:lo_row + ext, :] + s2[lo_row - 4:lo_row - 4 + ext, :]
    s3[lo_row:lo_row + ext, :] = a3[:, gdim:]
    a4 = s3[lo_row:lo_row + ext, :] + s3[lo_row - 8:lo_row - 8 + ext, :]
    sums = (a1[POOL_HALO:, :gdim], a2[POOL_HALO:, :gdim], a3[POOL_HALO:, :gdim], a4[POOL_HALO:, :])

    frame = first_frame + lax.broadcasted_iota(jnp.int32, (rows, 1), 0) + 1
    ys = []
    for g, win in enumerate(POOL_WINDOWS):
        cols = slice(g * gdim, (g + 1) * gdim)
        inv_count = 1.0 / jnp.minimum(frame, win).astype(F32)
        pooled = sums[g] * inv_count - u[:, cols]
        mixed = _dot(pooled.astype(BF16), gw_ref[cols, :]) + gb_ref[:, cols]
        gate = _dot(normed, pin_ref[:, d_model + g * gdim:d_model + (g + 1) * gdim])
        ys.append((mixed * ps_ref[:, cols] * _silu(gate)).astype(BF16))
        yield
    y = jnp.concatenate(ys, axis=-1)
    yield
    return x + _dot(y, pout_ref[...])


def _gla_layer(h, carry, w, head_k, head_v):
    nw_ref, gint_ref, gkw_ref, gkb_ref, hn_ref, _ = w
    rows, d_model = h.shape
    n_chunks = rows // CHUNK
    key_w = GLA_HEADS * head_k
    heads = range(GLA_HEADS)
    ksl = [slice(hd * head_k, (hd + 1) * head_k) for hd in heads]
    vsl = [slice(hd * head_v, (hd + 1) * head_v) for hd in heads]
    csl = [slice(n * CHUNK, (n + 1) * CHUNK) for n in range(n_chunks)]

    n2 = _rms_norm(h, nw_ref[1:2, :]).astype(BF16)
    qkvg_w = 2 * (key_w + d_model)
    glow_t = _dot_nt(gint_ref[qkvg_w:, :], n2).astype(BF16)
    qk = _dot_nt(n2, gint_ref[:2 * key_w, :])
    z = _dot_tn(glow_t, gkw_ref[...]) + gkb_ref[...]
    yield

    v = _dot_nt(n2, gint_ref[2 * key_w:2 * key_w + d_model, :]).astype(BF16)
    lg = _log_sigmoid(z) * (1.0 / GLA_GATE_NORMALIZER)
    lg_hi, lg_lo = _split_bf16(lg)
    r = lax.broadcasted_iota(jnp.int32, (rows, rows), 0)
    c = lax.broadcasted_iota(jnp.int32, (rows, rows), 1)
    same_chunk = (r // CHUNK) == (c // CHUNK)
    causal = c <= r
    tril = jnp.where(same_chunk & causal, 1.0, 0.0).astype(BF16)
    cum = _dot(tril, lg_hi) + _dot(tril, lg_lo)
    yield

    totals = [cum[cs.stop - 1:cs.stop, :] for cs in csl]
    tot = jnp.concatenate([jnp.broadcast_to(t, (CHUNK, key_w)) for t in totals], axis=0)
    e_pos = jnp.exp(cum)
    e_neg = jnp.exp(-cum)
    q = qk[:, :key_w] * (head_k ** -0.5)
    k = qk[:, key_w:]
    q_pos = (q * e_pos).astype(BF16)
    q_neg = (q * e_neg).astype(BF16)
    k_neg = (k * e_neg).astype(BF16)
    k_pos = (k * e_pos).astype(BF16)
    k_dec = (k * jnp.exp(tot - cum)).astype(BF16)
    decay_t = [[jnp.broadcast_to(jnp.exp(t[:, ks]), (V7X_LANES, head_k)).T for t in totals]
               for ks in ksl]
    yield

    fwd, bwd, upd, scores, entering, outs = {}, {}, {}, {}, {}, {}

    def state_free_matmuls(hd):
        ks, vs = ksl[hd], vsl[hd]
        fwd[hd] = _dot_nt(q_pos[:, ks], k_neg[:, ks])
        bwd[hd] = _dot_nt(q_neg[:, ks], k_pos[:, ks])
        upd[hd] = [_dot_tn(k_dec[cs, ks], v[cs, vs]) for cs in csl]

    def vector_work(hd):
        scores[hd] = jnp.where(same_chunk, jnp.where(causal, fwd[hd], bwd[hd]), 0.0).astype(BF16)
        st = carry["states"][hd]
        entering[hd] = []
        for n in range(n_chunks):
            entering[hd].append(st.astype(BF16))
            d_n = jnp.concatenate([decay_t[hd][n]] * (head_v // V7X_LANES), axis=-1)
            st = st * d_n + upd[hd][n]
        carry["states"][hd] = st

    def state_matmuls(hd):
        inter = [_dot(q_pos[cs, ksl[hd]], entering[hd][n]) for n, cs in enumerate(csl)]
        o_h = _dot(scores[hd], v[:, vsl[hd]]) + jnp.concatenate(inter, axis=0)
        o_h = o_h * lax.rsqrt(jnp.mean(o_h * o_h, axis=-1, keepdims=True) + RMS_EPS) * hn_ref[:, vsl[hd]]
        gate_rows = slice(2 * key_w + d_model + vsl[hd].start, 2 * key_w + d_model + vsl[hd].stop)
        outs[hd] = (o_h * _silu(_dot_nt(n2, gint_ref[gate_rows, :]))).astype(BF16)

    for hd in heads:
        state_free_matmuls(hd)
    yield
    for stage in (vector_work, state_matmuls):
        for hd in heads:
            stage(hd)
            yield
    return jnp.concatenate([outs[hd] for hd in heads], axis=-1)


def _sub_block(x_ref, o_ref, rows, first_frame, carry, pool_w, gla_w, fnw_ref, pool_scratch,
               head_k, head_v):
    h = yield from _pool_layer(x_ref[0, rows, :], carry, first_frame, pool_w, *pool_scratch)
    yield
    y2 = yield from _gla_layer(h, carry, gla_w, head_k, head_v)
    n_rows = rows.stop - rows.start
    piece = n_rows // TAIL_PIECES
    for p in range(TAIL_PIECES):
        part = slice(p * piece, (p + 1) * piece)
        h2 = h[part, :] + _dot(y2[part, :], gla_w[-1][...])
        o_ref[0, rows.start + p * piece:rows.start + (p + 1) * piece, :] = _rms_norm(h2, fnw_ref[...])


def _load_weights(pairs, stage, sems):
    chunks = []
    for src, dst in pairs:
        n_rows, n_cols = src.shape
        for r0 in range(0, n_rows, WEIGHT_ROWS):
            chunks.append((src, dst, r0, min(WEIGHT_ROWS, n_rows - r0), n_cols))
    ahead = WEIGHT_SLOTS - 1

    def copy(i):
        src, _, r0, n, c = chunks[i]
        slot = i % WEIGHT_SLOTS
        return pltpu.make_async_copy(src.at[r0:r0 + n, :], stage.at[slot, 0:n, 0:c], sems.at[slot])

    for i in range(min(ahead, len(chunks))):
        copy(i).start()
    for i, (_, dst, r0, n, c) in enumerate(chunks):
        if i + ahead < len(chunks):
            copy(i + ahead).start()
        copy(i).wait()
        dst[r0:r0 + n, :] = stage[i % WEIGHT_SLOTS, 0:n, 0:c].astype(BF16)


def _fused_kernel(x_ref, nw_ref, gb_ref, ps_ref, gkw_ref, gkb_ref, hn_ref, fnw_ref,
                  pin_hbm, gw_hbm, pout_hbm, gint_hbm, gout_hbm,
                  o_ref,
                  pin_ref, gw_ref, pout_ref, gint_ref, gout_ref, stage, sems,
                  uext, s1, s2, s3, halo, state,
                  *, ts, d_model, head_k, head_v):
    j = pl.program_id(1)
    n_sub = ts // SUB_ROWS
    lo_row = V7X_SUBLANES

    @pl.when(jnp.logical_and(pl.program_id(0) == 0, j == 0))
    def _():
        _load_weights([(pin_hbm, pin_ref), (gw_hbm, gw_ref), (pout_hbm, pout_ref),
                       (gint_hbm, gint_ref), (gout_hbm, gout_ref)], stage, sems)
        for buf in (uext, s1, s2, s3):
            buf[:, 0:lo_row, :] = jnp.zeros((n_sub, lo_row, buf.shape[2]), F32)

    @pl.when(j == 0)
    def _():
        halo[...] = jnp.zeros_like(halo)
        state[...] = jnp.zeros_like(state)

    pool_w = (nw_ref, pin_ref, gw_ref, gb_ref, ps_ref, pout_ref)
    gla_w = (nw_ref, gint_ref, gkw_ref, gkb_ref, hn_ref, gout_ref)

    carry = {"u_tail": halo[...], "states": [state[hd] for hd in range(GLA_HEADS)]}
    blocks = [
        _sub_block(x_ref, o_ref, slice(s * SUB_ROWS, (s + 1) * SUB_ROWS), j * ts + s * SUB_ROWS,
                   carry, pool_w, gla_w, fnw_ref, (uext.at[s], s1.at[s], s2.at[s], s3.at[s]),
                   head_k, head_v)
        for s in range(n_sub)
    ]
    while blocks:
        for blk in list(blocks):
            try:
                next(blk)
            except StopIteration:
                blocks.remove(blk)
    halo[...] = carry["u_tail"]
    for hd in range(GLA_HEADS):
        state[hd] = carry["states"][hd]


def _resident(shape):
    zeros = (0,) * len(shape)
    return pl.BlockSpec(shape, lambda b, j: zeros, pipeline_mode=pl.Buffered(1))


def kernel(x, norm_w, pool_in_w, pool_group_w, pool_group_b, pool_scale, pool_out_w,
           gla_in_w, gla_gk_w, gla_gk_b, gla_head_norm_w, gla_out_w, final_norm_w):
    batch, seq, d_model = x.shape
    ts = SEQ_TILE
    assert seq % ts == 0 and ts % SUB_ROWS == 0 and SUB_ROWS % CHUNK == 0
    assert norm_w.shape[0] == 2 and pool_in_w.shape[0] == 1 and gla_in_w.shape[0] == 1
    key_w = gla_gk_w.shape[-1]
    head_k = key_w // GLA_HEADS
    head_v = d_model // GLA_HEADS
    n_groups = len(POOL_WINDOWS)
    gdim = d_model // n_groups
    qkvg_w = 2 * key_w + 2 * d_model
    assert gla_in_w.shape[-1] == qkvg_w + GLA_GATE_RANK

    small = (norm_w, pool_group_b[0].reshape(1, d_model), pool_scale[0].reshape(1, d_model),
             gla_gk_w[0].astype(BF16), gla_gk_b[0].reshape(1, key_w),
             jnp.tile(gla_head_norm_w[0], GLA_HEADS).reshape(1, d_model),
             final_norm_w.reshape(1, d_model))
    big = (pool_in_w[0], pool_group_w[0].reshape(d_model, gdim), pool_out_w[0],
           gla_in_w[0].T, gla_out_w[0])
    assert all(w.shape[1] <= WEIGHT_COLS for w in big)

    operands = (x,) + small + big
    tile = pl.BlockSpec((1, ts, d_model), lambda b, j: (b, j, 0))
    in_specs = ([tile] + [_resident(op.shape) for op in small]
                + [pl.BlockSpec(memory_space=pl.ANY)] * len(big))
    n_sub = ts // SUB_ROWS
    rows = SUB_ROWS + POOL_OFF
    body = functools.partial(_fused_kernel, ts=ts, d_model=d_model, head_k=head_k, head_v=head_v)
    return pl.pallas_call(
        body,
        grid=(batch, seq // ts),
        in_specs=in_specs,
        out_specs=tile,
        out_shape=jax.ShapeDtypeStruct(x.shape, x.dtype),
        scratch_shapes=[
            pltpu.VMEM(pool_in_w.shape[1:], BF16),
            pltpu.VMEM((d_model, gdim), BF16),
            pltpu.VMEM(pool_out_w.shape[1:], BF16),
            pltpu.VMEM((gla_in_w.shape[2], d_model), BF16),
            pltpu.VMEM(gla_out_w.shape[1:], BF16),
            pltpu.VMEM((WEIGHT_SLOTS, WEIGHT_ROWS, WEIGHT_COLS), F32),
            pltpu.SemaphoreType.DMA((WEIGHT_SLOTS,)),
            pltpu.VMEM((n_sub, rows, d_model), F32),
            pltpu.VMEM((n_sub, rows, d_model - gdim), F32),
            pltpu.VMEM((n_sub, rows, d_model - 2 * gdim), F32),
            pltpu.VMEM((n_sub, rows, d_model - 3 * gdim), F32),
            pltpu.VMEM((POOL_HALO, d_model), F32),
            pltpu.VMEM((GLA_HEADS, head_k, head_v), F32),
        ],
        compiler_params=pltpu.CompilerParams(
            dimension_semantics=("arbitrary", "arbitrary"),
            vmem_limit_bytes=V7X_VMEM_LIMIT_BYTES,
        ),
        name="pool_gla_trunk",
    )(*operands)
```

```python
import functools

import jax
import jax.numpy as jnp
from jax import lax
from jax.experimental import pallas as pl
from jax.experimental.pallas import tpu as pltpu

F32 = jnp.float32
BF16 = jnp.bfloat16

RMS_EPS = 1e-6
CHUNK = 64
POOL_WINDOWS = (2, 4, 8, 16)
GLA_HEADS = 4
GLA_GATE_NORMALIZER = 16.0
GLA_GATE_RANK = 16

V7X_LANES = 128
V7X_SUBLANES = 8
V7X_VMEM_LIMIT_BYTES = 56 * 1024 * 1024

SEQ_TILE = 512
SUB_ROWS = 256
WEIGHT_ROWS = 512
WEIGHT_COLS = 2048
WEIGHT_SLOTS = 3
POOL_HALO = max(POOL_WINDOWS)
POOL_OFF = POOL_HALO + V7X_SUBLANES


def _dot(a, b):
    return jnp.dot(a, b, preferred_element_type=F32)


def _dot_nt(a, b):
    return lax.dot_general(a, b, (((1,), (1,)), ((), ())), preferred_element_type=F32)


def _dot_tn(a, b):
    return lax.dot_general(a, b, (((0,), (0,)), ((), ())), preferred_element_type=F32)


def _rms_norm(x, w):
    return x * lax.rsqrt(jnp.mean(x * x, axis=-1, keepdims=True) + RMS_EPS) * w


def _silu(x):
    return x / (1.0 + jnp.exp(-x))


def _log_sigmoid(z):
    return jnp.minimum(z, 0.0) - jnp.log(1.0 + jnp.exp(-jnp.abs(z)))


def _split_bf16(x):
    hi = x.astype(BF16)
    lo = (x - hi.astype(F32)).astype(BF16)
    return hi, lo


def _pool_layer(x, carry, first_frame, w, uext, s1, s2, s3):
    nw_ref, pin_ref, gw_ref, gb_ref, ps_ref, pout_ref = w
    rows, d_model = x.shape
    gdim = d_model // len(POOL_WINDOWS)
    ext = rows + POOL_HALO
    lo_row = V7X_SUBLANES

    normed = _rms_norm(x, nw_ref[0:1, :]).astype(BF16)
    u = _dot(normed, pin_ref[:, :d_model])
    uext[lo_row:POOL_OFF, :] = carry["u_tail"]
    uext[POOL_OFF:POOL_OFF + rows, :] = u
    carry["u_tail"] = u[rows - POOL_HALO:, :]
    yield
    gate = _dot(normed, pin_ref[:, d_model:])

    a1 = uext[lo_row:lo_row + ext, :] + uext[lo_row - 1:lo_row - 1 + ext, :]
    s1[lo_row:lo_row + ext, :] = a1[:, gdim:]
    a2 = s1[lo_row:lo_row + ext, :] + s1[lo_row - 2:lo_row - 2 + ext, :]
    s2[lo_row:lo_row + ext, :] = a2[:, gdim:]
    a3 = s2[lo_row:lo_row + ext, :] + s2[lo_row - 4:lo_row - 4 + ext, :]
    s3[lo_row:lo_row + ext, :] = a3[:, gdim:]
    a4 = s3[lo_row:lo_row + ext, :] + s3[lo_row - 8:lo_row - 8 + ext, :]
    sums = (a1[POOL_HALO:, :gdim], a2[POOL_HALO:, :gdim], a3[POOL_HALO:, :gdim], a4[POOL_HALO:, :])

    frame = first_frame + lax.broadcasted_iota(jnp.int32, (rows, 1), 0) + 1
    ys = []
    for g, win in enumerate(POOL_WINDOWS):
        cols = slice(g * gdim, (g + 1) * gdim)
        inv_count = 1.0 / jnp.minimum(frame, win).astype(F32)
        pooled = sums[g] * inv_count - u[:, cols]
        mixed = _dot(pooled.astype(BF16), gw_ref[cols, :]) + gb_ref[:, cols]
        ys.append(mixed * ps_ref[:, cols] * _silu(gate[:, cols]))
        yield
    y = jnp.concatenate(ys, axis=-1).astype(BF16)
    yield
    return x + _dot(y, pout_ref[...])


def _gla_layer(h, carry, w, head_k, head_v):
    nw_ref, gint_ref, gkw_ref, gkb_ref, hn_ref, _ = w
    rows, d_model = h.shape
    n_chunks = rows // CHUNK
    key_w = GLA_HEADS * head_k
    heads = range(GLA_HEADS)
    ksl = [slice(hd * head_k, (hd + 1) * head_k) for hd in heads]
    vsl = [slice(hd * head_v, (hd + 1) * head_v) for hd in heads]
    csl = [slice(n * CHUNK, (n + 1) * CHUNK) for n in range(n_chunks)]

    n2 = _rms_norm(h, nw_ref[1:2, :]).astype(BF16)
    qkvg_w = 2 * (key_w + d_model)
    glow_t = _dot_nt(gint_ref[qkvg_w:, :], n2).astype(BF16)
    qk = _dot_nt(n2, gint_ref[:2 * key_w, :])
    z = _dot_tn(glow_t, gkw_ref[...]) + gkb_ref[...]
    yield

    vg = _dot_nt(n2, gint_ref[2 * key_w:qkvg_w, :])
    lg = _log_sigmoid(z) * (1.0 / GLA_GATE_NORMALIZER)
    lg_hi, lg_lo = _split_bf16(lg)
    r = lax.broadcasted_iota(jnp.int32, (rows, rows), 0)
    c = lax.broadcasted_iota(jnp.int32, (rows, rows), 1)
    same_chunk = (r // CHUNK) == (c // CHUNK)
    causal = c <= r
    tril = jnp.where(same_chunk & causal, 1.0, 0.0).astype(BF16)
    cum = _dot(tril, lg_hi) + _dot(tril, lg_lo)
    yield

    totals = [cum[cs.stop - 1:cs.stop, :] for cs in csl]
    tot = jnp.concatenate([jnp.broadcast_to(t, (CHUNK, key_w)) for t in totals], axis=0)
    e_pos = jnp.exp(cum)
    e_neg = jnp.exp(-cum)
    q = qk[:, :key_w] * (head_k ** -0.5)
    k = qk[:, key_w:]
    q_pos = (q * e_pos).astype(BF16)
    q_neg = (q * e_neg).astype(BF16)
    k_neg = (k * e_neg).astype(BF16)
    k_pos = (k * e_pos).astype(BF16)
    k_dec = (k * jnp.exp(tot - cum)).astype(BF16)
    v = vg[:, :d_model].astype(BF16)
    gate_act = _silu(vg[:, d_model:])
    decay_t = [[jnp.broadcast_to(jnp.exp(t[:, ks]), (V7X_LANES, head_k)).T for t in totals]
               for ks in ksl]
    yield

    fwd, bwd, upd, scores, entering, outs = {}, {}, {}, {}, {}, {}

    def state_free_matmuls(hd):
        ks, vs = ksl[hd], vsl[hd]
        fwd[hd] = _dot_nt(q_pos[:, ks], k_neg[:, ks])
        bwd[hd] = _dot_nt(q_neg[:, ks], k_pos[:, ks])
        upd[hd] = [_dot_tn(k_dec[cs, ks], v[cs, vs]) for cs in csl]

    def vector_work(hd):
        scores[hd] = jnp.where(same_chunk, jnp.where(causal, fwd[hd], bwd[hd]), 0.0).astype(BF16)
        st = carry["states"][hd]
        entering[hd] = []
        for n in range(n_chunks):
            entering[hd].append(st.astype(BF16))
            d_n = jnp.concatenate([decay_t[hd][n]] * (head_v // V7X_LANES), axis=-1)
            st = st * d_n + upd[hd][n]
        carry["states"][hd] = st

    def state_matmuls(hd):
        inter = [_dot(q_pos[cs, ksl[hd]], entering[hd][n]) for n, cs in enumerate(csl)]
        o_h = _dot(scores[hd], v[:, vsl[hd]]) + jnp.concatenate(inter, axis=0)
        outs[hd] = o_h * lax.rsqrt(jnp.mean(o_h * o_h, axis=-1, keepdims=True) + RMS_EPS) * hn_ref[:, vsl[hd]]

    for hd in heads:
        state_free_matmuls(hd)
    yield
    for stage in (vector_work, state_matmuls):
        for hd in heads:
            stage(hd)
            yield
    return (jnp.concatenate([outs[hd] for hd in heads], axis=-1) * gate_act).astype(BF16)


def _sub_block(x_ref, o_ref, rows, first_frame, carry, pool_w, gla_w, fnw_ref, pool_scratch,
               head_k, head_v):
    h = yield from _pool_layer(x_ref[0, rows, :], carry, first_frame, pool_w, *pool_scratch)
    yield
    y2 = yield from _gla_layer(h, carry, gla_w, head_k, head_v)
    h2 = h + _dot(y2, gla_w[-1][...])
    o_ref[0, rows, :] = _rms_norm(h2, fnw_ref[...])


def _load_weights(pairs, stage, sems):
    chunks = []
    for src, dst in pairs:
        n_rows, n_cols = src.shape
        for r0 in range(0, n_rows, WEIGHT_ROWS):
            chunks.append((src, dst, r0, min(WEIGHT_ROWS, n_rows - r0), n_cols))
    ahead = WEIGHT_SLOTS - 1

    def copy(i):
        src, _, r0, n, c = chunks[i]
        slot = i % WEIGHT_SLOTS
        return pltpu.make_async_copy(src.at[r0:r0 + n, :], stage.at[slot, 0:n, 0:c], sems.at[slot])

    for i in range(min(ahead, len(chunks))):
        copy(i).start()
    for i, (_, dst, r0, n, c) in enumerate(chunks):
        if i + ahead < len(chunks):
            copy(i + ahead).start()
        copy(i).wait()
        dst[r0:r0 + n, :] = stage[i % WEIGHT_SLOTS, 0:n, 0:c].astype(BF16)


def _fused_kernel(x_ref, nw_ref, gb_ref, ps_ref, gkw_ref, gkb_ref, hn_ref, fnw_ref,
                  pin_hbm, gw_hbm, pout_hbm, gint_hbm, gout_hbm,
                  o_ref,
                  pin_ref, gw_ref, pout_ref, gint_ref, gout_ref, stage, sems,
                  uext, s1, s2, s3, halo, state,
                  *, ts, d_model, head_k, head_v):
    j = pl.program_id(1)
    n_sub = ts // SUB_ROWS
    lo_row = V7X_SUBLANES

    @pl.when(jnp.logical_and(pl.program_id(0) == 0, j == 0))
    def _():
        _load_weights([(pin_hbm, pin_ref), (gw_hbm, gw_ref), (pout_hbm, pout_ref),
                       (gint_hbm, gint_ref), (gout_hbm, gout_ref)], stage, sems)
        for buf in (uext, s1, s2, s3):
            buf[:, 0:lo_row, :] = jnp.zeros((n_sub, lo_row, buf.shape[2]), F32)

    @pl.when(j == 0)
    def _():
        halo[...] = jnp.zeros_like(halo)
        state[...] = jnp.zeros_like(state)

    pool_w = (nw_ref, pin_ref, gw_ref, gb_ref, ps_ref, pout_ref)
    gla_w = (nw_ref, gint_ref, gkw_ref, gkb_ref, hn_ref, gout_ref)

    carry = {"u_tail": halo[...], "states": [state[hd] for hd in range(GLA_HEADS)]}
    blocks = [
        _sub_block(x_ref, o_ref, slice(s * SUB_ROWS, (s + 1) * SUB_ROWS), j * ts + s * SUB_ROWS,
                   carry, pool_w, gla_w, fnw_ref, (uext.at[s], s1.at[s], s2.at[s], s3.at[s]),
                   head_k, head_v)
        for s in range(n_sub)
    ]
    while blocks:
        for blk in list(blocks):
            try:
                next(blk)
            except StopIteration:
                blocks.remove(blk)
    halo[...] = carry["u_tail"]
    for hd in range(GLA_HEADS):
        state[hd] = carry["states"][hd]


def _resident(shape):
    zeros = (0,) * len(shape)
    return pl.BlockSpec(shape, lambda b, j: zeros, pipeline_mode=pl.Buffered(1))


def kernel(x, norm_w, pool_in_w, pool_group_w, pool_group_b, pool_scale, pool_out_w,
           gla_in_w, gla_gk_w, gla_gk_b, gla_head_norm_w, gla_out_w, final_norm_w):
    batch, seq, d_model = x.shape
    ts = SEQ_TILE
    assert seq % ts == 0 and ts % SUB_ROWS == 0 and SUB_ROWS % CHUNK == 0
    assert norm_w.shape[0] == 2 and pool_in_w.shape[0] == 1 and gla_in_w.shape[0] == 1
    key_w = gla_gk_w.shape[-1]
    head_k = key_w // GLA_HEADS
    head_v = d_model // GLA_HEADS
    n_groups = len(POOL_WINDOWS)
    gdim = d_model // n_groups
    qkvg_w = 2 * key_w + 2 * d_model
    assert gla_in_w.shape[-1] == qkvg_w + GLA_GATE_RANK

    small = (norm_w, pool_group_b[0].reshape(1, d_model), pool_scale[0].reshape(1, d_model),
             gla_gk_w[0].astype(BF16), gla_gk_b[0].reshape(1, key_w),
             jnp.tile(gla_head_norm_w[0], GLA_HEADS).reshape(1, d_model),
             final_norm_w.reshape(1, d_model))
    big = (pool_in_w[0], pool_group_w[0].reshape(d_model, gdim), pool_out_w[0],
           gla_in_w[0].T, gla_out_w[0])
    assert all(w.shape[1] <= WEIGHT_COLS for w in big)

    operands = (x,) + small + big
    tile = pl.BlockSpec((1, ts, d_model), lambda b, j: (b, j, 0))
    in_specs = ([tile] + [_resident(op.shape) for op in small]
                + [pl.BlockSpec(memory_space=pl.ANY)] * len(big))
    n_sub = ts // SUB_ROWS
    rows = SUB_ROWS + POOL_OFF
    body = functools.partial(_fused_kernel, ts=ts, d_model=d_model, head_k=head_k, head_v=head_v)
    return pl.pallas_call(
        body,
        grid=(batch, seq // ts),
        in_specs=in_specs,
        out_specs=tile,
        out_shape=jax.ShapeDtypeStruct(x.shape, x.dtype),
        scratch_shapes=[
            pltpu.VMEM(pool_in_w.shape[1:], BF16),
            pltpu.VMEM((d_model, gdim), BF16),
            pltpu.VMEM(pool_out_w.shape[1:], BF16),
            pltpu.VMEM((gla_in_w.shape[2], d_model), BF16),
            pltpu.VMEM(gla_out_w.shape[1:], BF16),
            pltpu.VMEM((WEIGHT_SLOTS, WEIGHT_ROWS, WEIGHT_COLS), F32),
            pltpu.SemaphoreType.DMA((WEIGHT_SLOTS,)),
            pltpu.VMEM((n_sub, rows, d_model), F32),
            pltpu.VMEM((n_sub, rows, d_model - gdim), F32),
            pltpu.VMEM((n_sub, rows, d_model - 2 * gdim), F32),
            pltpu.VMEM((n_sub, rows, d_model - 3 * gdim), F32),
            pltpu.VMEM((POOL_HALO, d_model), F32),
            pltpu.VMEM((GLA_HEADS, head_k, head_v), F32),
        ],
        compiler_params=pltpu.CompilerParams(
            dimension_semantics=("arbitrary", "arbitrary"),
            vmem_limit_bytes=V7X_VMEM_LIMIT_BYTES,
        ),
        name="pool_gla_trunk",
    )(*operands)
```

```python
import functools

import jax
import jax.numpy as jnp
from jax import lax
from jax.experimental import pallas as pl
from jax.experimental.pallas import tpu as pltpu

F32 = jnp.float32
BF16 = jnp.bfloat16

RMS_EPS = 1e-6
CHUNK = 64
POOL_WINDOWS = (2, 4, 8, 16)
GLA_HEADS = 4
GLA_GATE_NORMALIZER = 16.0
GLA_GATE_RANK = 16

V7X_LANES = 128
V7X_SUBLANES = 8
V7X_VMEM_LIMIT_BYTES = 56 * 1024 * 1024

SEQ_TILE = 512
SUB_ROWS = 256
WEIGHT_ROWS = 1024
WEIGHT_COLS = 1024
WEIGHT_SLOTS = 3
POOL_HALO = max(POOL_WINDOWS)
POOL_OFF = POOL_HALO + V7X_SUBLANES


def _dot(a, b):
    return jnp.dot(a, b, preferred_element_type=F32)


def _dot_nt(a, b):
    return lax.dot_general(a, b, (((1,), (1,)), ((), ())), preferred_element_type=F32)


def _dot_tn(a, b):
    return lax.dot_general(a, b, (((0,), (0,)), ((), ())), preferred_element_type=F32)


def _rms_norm(x, w):
    return x * lax.rsqrt(jnp.mean(x * x, axis=-1, keepdims=True) + RMS_EPS) * w


def _silu(x):
    return x / (1.0 + jnp.exp(-x))


def _log_sigmoid(z):
    return jnp.minimum(z, 0.0) - jnp.log(1.0 + jnp.exp(-jnp.abs(z)))


def _split_bf16(x):
    hi = x.astype(BF16)
    lo = (x - hi.astype(F32)).astype(BF16)
    return hi, lo


def _pool_layer(x, carry, first_frame, w, uext, s1, s2, s3):
    nw_ref, pin_ref, gw_ref, gb_ref, ps_ref, pout_ref = w
    rows, d_model = x.shape
    gdim = d_model // len(POOL_WINDOWS)
    ext = rows + POOL_HALO
    lo_row = V7X_SUBLANES

    normed = _rms_norm(x, nw_ref[0:1, :]).astype(BF16)
    u = _dot(normed, pin_ref[:, :d_model])
    uext[lo_row:POOL_OFF, :] = carry["u_tail"]
    uext[POOL_OFF:POOL_OFF + rows, :] = u
    carry["u_tail"] = u[rows - POOL_HALO:, :]
    yield
    gate = _dot(normed, pin_ref[:, d_model:])

    a1 = uext[lo_row:lo_row + ext, :] + uext[lo_row - 1:lo_row - 1 + ext, :]
    s1[lo_row:lo_row + ext, :] = a1[:, gdim:]
    a2 = s1[lo_row:lo_row + ext, :] + s1[lo_row - 2:lo_row - 2 + ext, :]
    s2[lo_row:lo_row + ext, :] = a2[:, gdim:]
    a3 = s2[lo_row:lo_row + ext, :] + s2[lo_row - 4:lo_row - 4 + ext, :]
    s3[lo_row:lo_row + ext, :] = a3[:, gdim:]
    a4 = s3[lo_row:lo_row + ext, :] + s3[lo_row - 8:lo_row - 8 + ext, :]
    sums = (a1[POOL_HALO:, :gdim], a2[POOL_HALO:, :gdim], a3[POOL_HALO:, :gdim], a4[POOL_HALO:, :])

    frame = first_frame + lax.broadcasted_iota(jnp.int32, (rows, 1), 0) + 1
    ys = []
    for g, win in enumerate(POOL_WINDOWS):
        cols = slice(g * gdim, (g + 1) * gdim)
        inv_count = 1.0 / jnp.minimum(frame, win).astype(F32)
        pooled = sums[g] * inv_count - u[:, cols]
        mixed = _dot(pooled.astype(BF16), gw_ref[cols, :]) + gb_ref[:, cols]
        ys.append(mixed * ps_ref[:, cols] * _silu(gate[:, cols]))
        yield
    y = jnp.concatenate(ys, axis=-1).astype(BF16)
    yield
    return x + _dot(y, pout_ref[...])


def _gla_layer(h, carry, w, head_k, head_v):
    nw_ref, gint_ref, gkw_ref, gkb_ref, hn_ref, _ = w
    rows, d_model = h.shape
    n_chunks = rows // CHUNK
    key_w = GLA_HEADS * head_k
    heads = range(GLA_HEADS)
    ksl = [slice(hd * head_k, (hd + 1) * head_k) for hd in heads]
    vsl = [slice(hd * head_v, (hd + 1) * head_v) for hd in heads]
    csl = [slice(n * CHUNK, (n + 1) * CHUNK) for n in range(n_chunks)]

    n2 = _rms_norm(h, nw_ref[1:2, :]).astype(BF16)
    qkvg_w = 2 * (key_w + d_model)
    glow_t = _dot_nt(gint_ref[qkvg_w:, :], n2).astype(BF16)
    qk = _dot_nt(n2, gint_ref[:2 * key_w, :])
    z = _dot_tn(glow_t, gkw_ref[...]) + gkb_ref[...]
    yield

    vg = _dot_nt(n2, gint_ref[2 * key_w:qkvg_w, :])
    lg = _log_sigmoid(z) * (1.0 / GLA_GATE_NORMALIZER)
    lg_hi, lg_lo = _split_bf16(lg)
    r = lax.broadcasted_iota(jnp.int32, (rows, rows), 0)
    c = lax.broadcasted_iota(jnp.int32, (rows, rows), 1)
    same_chunk = (r // CHUNK) == (c // CHUNK)
    causal = c <= r
    tril = jnp.where(same_chunk & causal, 1.0, 0.0).astype(BF16)
    cum = _dot(tril, lg_hi) + _dot(tril, lg_lo)
    yield

    totals = [cum[cs.stop - 1:cs.stop, :] for cs in csl]
    tot = jnp.concatenate([jnp.broadcast_to(t, (CHUNK, key_w)) for t in totals], axis=0)
    e_pos = jnp.exp(cum)
    e_neg = jnp.exp(-cum)
    q = qk[:, :key_w] * (head_k ** -0.5)
    k = qk[:, key_w:]
    q_pos = (q * e_pos).astype(BF16)
    q_neg = (q * e_neg).astype(BF16)
    k_neg = (k * e_neg).astype(BF16)
    k_pos = (k * e_pos).astype(BF16)
    k_dec = (k * jnp.exp(tot - cum)).astype(BF16)
    v = vg[:, :d_model].astype(BF16)
    gate_act = _silu(vg[:, d_model:])
    decay_t = [[jnp.broadcast_to(jnp.exp(t[:, ks]), (V7X_LANES, head_k)).T for t in totals]
               for ks in ksl]
    yield

    fwd, bwd, upd, scores, entering, outs = {}, {}, {}, {}, {}, {}

    def state_free_matmuls(hd):
        ks, vs = ksl[hd], vsl[hd]
        fwd[hd] = _dot_nt(q_pos[:, ks], k_neg[:, ks])
        bwd[hd] = _dot_nt(q_neg[:, ks], k_pos[:, ks])
        upd[hd] = [_dot_tn(k_dec[cs, ks], v[cs, vs]) for cs in csl]

    def vector_work(hd):
        scores[hd] = jnp.where(same_chunk, jnp.where(causal, fwd[hd], bwd[hd]), 0.0).astype(BF16)
        st = carry["states"][hd]
        entering[hd] = []
        for n in range(n_chunks):
            entering[hd].append(st.astype(BF16))
            d_n = jnp.concatenate([decay_t[hd][n]] * (head_v // V7X_LANES), axis=-1)
            st = st * d_n + upd[hd][n]
        carry["states"][hd] = st

    def state_matmuls(hd):
        inter = [_dot(q_pos[cs, ksl[hd]], entering[hd][n]) for n, cs in enumerate(csl)]
        o_h = _dot(scores[hd], v[:, vsl[hd]]) + jnp.concatenate(inter, axis=0)
        outs[hd] = o_h * lax.rsqrt(jnp.mean(o_h * o_h, axis=-1, keepdims=True) + RMS_EPS) * hn_ref[:, vsl[hd]]

    for hd in heads:
        state_free_matmuls(hd)
    yield
    for stage in (vector_work, state_matmuls):
        for hd in heads:
            stage(hd)
            yield
    return (jnp.concatenate([outs[hd] for hd in heads], axis=-1) * gate_act).astype(BF16)


def _sub_block(x_ref, o_ref, rows, first_frame, carry, pool_w, gla_w, fnw_ref, pool_scratch,
               head_k, head_v):
    h = yield from _pool_layer(x_ref[0, rows, :], carry, first_frame, pool_w, *pool_scratch)
    yield
    y2 = yield from _gla_layer(h, carry, gla_w, head_k, head_v)
    h2 = h + _dot(y2, gla_w[-1][...])
    o_ref[0, rows, :] = _rms_norm(h2, fnw_ref[...])


def _load_weights(pairs, stage, sems):
    chunks = []
    for src, dst in pairs:
        n_rows, n_cols = src.shape
        for r0 in range(0, n_rows, WEIGHT_ROWS):
            for c0 in range(0, n_cols, WEIGHT_COLS):
                chunks.append((src, dst, r0, min(WEIGHT_ROWS, n_rows - r0),
                               c0, min(WEIGHT_COLS, n_cols - c0)))
    ahead = WEIGHT_SLOTS - 1

    def copy(i):
        src, _, r0, n, c0, c = chunks[i]
        slot = i % WEIGHT_SLOTS
        return pltpu.make_async_copy(src.at[r0:r0 + n, c0:c0 + c], stage.at[slot, 0:n, 0:c],
                                     sems.at[slot])

    for i in range(min(ahead, len(chunks))):
        copy(i).start()
    for i, (_, dst, r0, n, c0, c) in enumerate(chunks):
        if i + ahead < len(chunks):
            copy(i + ahead).start()
        copy(i).wait()
        dst[r0:r0 + n, c0:c0 + c] = stage[i % WEIGHT_SLOTS, 0:n, 0:c].astype(BF16)


def _fused_kernel(x_ref, nw_ref, gb_ref, ps_ref, gkw_ref, gkb_ref, hn_ref, fnw_ref,
                  pin_hbm, gw_hbm, pout_hbm, gint_hbm, gout_hbm,
                  o_ref,
                  pin_ref, gw_ref, pout_ref, gint_ref, gout_ref, stage, sems,
                  uext, s1, s2, s3, halo, state,
                  *, ts, d_model, head_k, head_v):
    j = pl.program_id(1)
    n_sub = ts // SUB_ROWS
    lo_row = V7X_SUBLANES

    @pl.when(jnp.logical_and(pl.program_id(0) == 0, j == 0))
    def _():
        _load_weights([(pin_hbm, pin_ref), (gw_hbm, gw_ref), (pout_hbm, pout_ref),
                       (gint_hbm, gint_ref), (gout_hbm, gout_ref)], stage, sems)
        for buf in (uext, s1, s2, s3):
            buf[:, 0:lo_row, :] = jnp.zeros((n_sub, lo_row, buf.shape[2]), F32)

    @pl.when(j == 0)
    def _():
        halo[...] = jnp.zeros_like(halo)
        state[...] = jnp.zeros_like(state)

    pool_w = (nw_ref, pin_ref, gw_ref, gb_ref, ps_ref, pout_ref)
    gla_w = (nw_ref, gint_ref, gkw_ref, gkb_ref, hn_ref, gout_ref)

    carry = {"u_tail": halo[...], "states": [state[hd] for hd in range(GLA_HEADS)]}
    blocks = [
        _sub_block(x_ref, o_ref, slice(s * SUB_ROWS, (s + 1) * SUB_ROWS), j * ts + s * SUB_ROWS,
                   carry, pool_w, gla_w, fnw_ref, (uext.at[s], s1.at[s], s2.at[s], s3.at[s]),
                   head_k, head_v)
        for s in range(n_sub)
    ]
    while blocks:
        for blk in list(blocks):
            try:
                next(blk)
            except StopIteration:
                blocks.remove(blk)
    halo[...] = carry["u_tail"]
    for hd in range(GLA_HEADS):
        state[hd] = carry["states"][hd]


def _resident(shape):
    zeros = (0,) * len(shape)
    return pl.BlockSpec(shape, lambda b, j: zeros, pipeline_mode=pl.Buffered(1))


def kernel(x, norm_w, pool_in_w, pool_group_w, pool_group_b, pool_scale, pool_out_w,
           gla_in_w, gla_gk_w, gla_gk_b, gla_head_norm_w, gla_out_w, final_norm_w):
    batch, seq, d_model = x.shape
    ts = SEQ_TILE
    assert seq % ts == 0 and ts % SUB_ROWS == 0 and SUB_ROWS % CHUNK == 0
    assert norm_w.shape[0] == 2 and pool_in_w.shape[0] == 1 and gla_in_w.shape[0] == 1
    key_w = gla_gk_w.shape[-1]
    head_k = key_w // GLA_HEADS
    head_v = d_model // GLA_HEADS
    n_groups = len(POOL_WINDOWS)
    gdim = d_model // n_groups
    qkvg_w = 2 * key_w + 2 * d_model
    assert gla_in_w.shape[-1] == qkvg_w + GLA_GATE_RANK

    small = (norm_w, pool_group_b[0].reshape(1, d_model), pool_scale[0].reshape(1, d_model),
             gla_gk_w[0].astype(BF16), gla_gk_b[0].reshape(1, key_w),
             jnp.tile(gla_head_norm_w[0], GLA_HEADS).reshape(1, d_model),
             final_norm_w.reshape(1, d_model))
    big = (pool_in_w[0], pool_group_w[0].reshape(d_model, gdim), pool_out_w[0],
           gla_in_w[0].T, gla_out_w[0])
    operands = (x,) + small + big
    tile = pl.BlockSpec((1, ts, d_model), lambda b, j: (b, j, 0))
    in_specs = ([tile] + [_resident(op.shape) for op in small]
                + [pl.BlockSpec(memory_space=pl.ANY)] * len(big))
    n_sub = ts // SUB_ROWS
    rows = SUB_ROWS + POOL_OFF
    body = functools.partial(_fused_kernel, ts=ts, d_model=d_model, head_k=head_k, head_v=head_v)
    return pl.pallas_call(
        body,
        grid=(batch, seq // ts),
        in_specs=in_specs,
        out_specs=tile,
        out_shape=jax.ShapeDtypeStruct(x.shape, x.dtype),
        scratch_shapes=[
            pltpu.VMEM(pool_in_w.shape[1:], BF16),
            pltpu.VMEM((d_model, gdim), BF16),
            pltpu.VMEM(pool_out_w.shape[1:], BF16),
            pltpu.VMEM((gla_in_w.shape[2], d_model), BF16),
            pltpu.VMEM(gla_out_w.shape[1:], BF16),
            pltpu.VMEM((WEIGHT_SLOTS, WEIGHT_ROWS, WEIGHT_COLS), F32),
            pltpu.SemaphoreType.DMA((WEIGHT_SLOTS,)),
            pltpu.VMEM((n_sub, rows, d_model), F32),
            pltpu.VMEM((n_sub, rows, d_model - gdim), F32),
            pltpu.VMEM((n_sub, rows, d_model - 2 * gdim), F32),
            pltpu.VMEM((n_sub, rows, d_model - 3 * gdim), F32),
            pltpu.VMEM((POOL_HALO, d_model), F32),
            pltpu.VMEM((GLA_HEADS, head_k, head_v), F32),
        ],
        compiler_params=pltpu.CompilerParams(
            dimension_semantics=("arbitrary", "arbitrary"),
            vmem_limit_bytes=V7X_VMEM_LIMIT_BYTES,
        ),
        name="pool_gla_trunk",
    )(*operands)
```

```python
import functools

import jax
import jax.numpy as jnp
from jax import lax
from jax.experimental import pallas as pl
from jax.experimental.pallas import tpu as pltpu

F32 = jnp.float32
BF16 = jnp.bfloat16

RMS_EPS = 1e-6
CHUNK = 64
POOL_WINDOWS = (2, 4, 8, 16)
GLA_HEADS = 4
GLA_GATE_NORMALIZER = 16.0
GLA_GATE_RANK = 16

V7X_LANES = 128
V7X_SUBLANES = 8
V7X_VMEM_LIMIT_BYTES = 56 * 1024 * 1024

SEQ_TILE = 512
SUB_ROWS = 256
WEIGHT_ROWS = 512
WEIGHT_COLS = 2048
WEIGHT_SLOTS = 3
POOL_HALO = max(POOL_WINDOWS)
POOL_OFF = POOL_HALO + V7X_SUBLANES


def _dot(a, b):
    return jnp.dot(a, b, preferred_element_type=F32)


def _dot_nt(a, b):
    return lax.dot_general(a, b, (((1,), (1,)), ((), ())), preferred_element_type=F32)


def _dot_tn(a, b):
    return lax.dot_general(a, b, (((0,), (0,)), ((), ())), preferred_element_type=F32)


def _rms_norm(x, w):
    return x * lax.rsqrt(jnp.mean(x * x, axis=-1, keepdims=True) + RMS_EPS) * w


def _silu(x):
    return x / (1.0 + jnp.exp(-x))


def _log_sigmoid(z):
    return jnp.minimum(z, 0.0) - jnp.log(1.0 + jnp.exp(-jnp.abs(z)))


def _split_bf16(x):
    hi = x.astype(BF16)
    lo = (x - hi.astype(F32)).astype(BF16)
    return hi, lo


def _pool_layer(x, carry, first_frame, w, uext, s1, s2, s3):
    nw_ref, pin_ref, gw_ref, gb_ref, ps_ref, pout_ref = w
    rows, d_model = x.shape
    gdim = d_model // len(POOL_WINDOWS)
    ext = rows + POOL_HALO
    lo_row = V7X_SUBLANES

    normed = _rms_norm(x, nw_ref[0:1, :]).astype(BF16)
    u = _dot(normed, pin_ref[:, :d_model])
    uext[lo_row:POOL_OFF, :] = carry["u_tail"]
    uext[POOL_OFF:POOL_OFF + rows, :] = u
    carry["u_tail"] = u[rows - POOL_HALO:, :]
    yield
    gate = _dot(normed, pin_ref[:, d_model:])

    a1 = uext[lo_row:lo_row + ext, :] + uext[lo_row - 1:lo_row - 1 + ext, :]
    s1[lo_row:lo_row + ext, :] = a1[:, gdim:]
    a2 = s1[lo_row:lo_row + ext, :] + s1[lo_row - 2:lo_row - 2 + ext, :]
    s2[lo_row:lo_row + ext, :] = a2[:, gdim:]
    a3 = s2[lo_row:lo_row + ext, :] + s2[lo_row - 4:lo_row - 4 + ext, :]
    s3[lo_row:lo_row + ext, :] = a3[:, gdim:]
    a4 = s3[lo_row:lo_row + ext, :] + s3[lo_row - 8:lo_row - 8 + ext, :]
    sums = (a1[POOL_HALO:, :gdim], a2[POOL_HALO:, :gdim], a3[POOL_HALO:, :gdim], a4[POOL_HALO:, :])

    frame = first_frame + lax.broadcasted_iota(jnp.int32, (rows, 1), 0) + 1
    ys = []
    for g, win in enumerate(POOL_WINDOWS):
        cols = slice(g * gdim, (g + 1) * gdim)
        inv_count = 1.0 / jnp.minimum(frame, win).astype(F32)
        pooled = sums[g] * inv_count - u[:, cols]
        mixed = _dot(pooled.astype(BF16), gw_ref[cols, :]) + gb_ref[g:g + 1, :]
        ys.append(mixed * ps_ref[:, cols] * _silu(gate[:, cols]))
        yield
    y = jnp.concatenate(ys, axis=-1).astype(BF16)
    yield
    return x + _dot(y, pout_ref[...])


def _gla_layer(h, carry, w, head_k, head_v):
    nw_ref, gint_ref, gkw_ref, gkb_ref, hn_ref, _ = w
    rows, d_model = h.shape
    n_chunks = rows // CHUNK
    key_w = GLA_HEADS * head_k
    heads = range(GLA_HEADS)
    ksl = [slice(hd * head_k, (hd + 1) * head_k) for hd in heads]
    vsl = [slice(hd * head_v, (hd + 1) * head_v) for hd in heads]
    csl = [slice(n * CHUNK, (n + 1) * CHUNK) for n in range(n_chunks)]

    n2 = _rms_norm(h, nw_ref[1:2, :]).astype(BF16)
    qkvg_w = 2 * (key_w + d_model)
    glow_t = _dot_nt(gint_ref[qkvg_w:, :], n2).astype(BF16)
    qk = _dot_nt(n2, gint_ref[:2 * key_w, :])
    z = _dot_tn(glow_t, gkw_ref[...].astype(BF16)) + gkb_ref[...]
    yield

    vg = _dot_nt(n2, gint_ref[2 * key_w:qkvg_w, :])
    lg = _log_sigmoid(z) * (1.0 / GLA_GATE_NORMALIZER)
    lg_hi, lg_lo = _split_bf16(lg)
    r = lax.broadcasted_iota(jnp.int32, (rows, rows), 0)
    c = lax.broadcasted_iota(jnp.int32, (rows, rows), 1)
    same_chunk = (r // CHUNK) == (c // CHUNK)
    causal = c <= r
    tril = jnp.where(same_chunk & causal, 1.0, 0.0).astype(BF16)
    cum = _dot(tril, lg_hi) + _dot(tril, lg_lo)
    yield

    totals = [cum[cs.stop - 1:cs.stop, :] for cs in csl]
    tot = jnp.concatenate([jnp.broadcast_to(t, (CHUNK, key_w)) for t in totals], axis=0)
    e_pos = jnp.exp(cum)
    e_neg = jnp.exp(-cum)
    q = qk[:, :key_w] * (head_k ** -0.5)
    k = qk[:, key_w:]
    q_pos = (q * e_pos).astype(BF16)
    q_neg = (q * e_neg).astype(BF16)
    k_neg = (k * e_neg).astype(BF16)
    k_pos = (k * e_pos).astype(BF16)
    k_dec = (k * jnp.exp(tot - cum)).astype(BF16)
    v = vg[:, :d_model].astype(BF16)
    gate_act = _silu(vg[:, d_model:])
    decay_t = [[jnp.broadcast_to(jnp.exp(t[:, ks]), (V7X_LANES, head_k)).T for t in totals]
               for ks in ksl]
    yield

    fwd, bwd, upd, scores, entering, outs = {}, {}, {}, {}, {}, {}

    def state_free_matmuls(hd):
        ks, vs = ksl[hd], vsl[hd]
        fwd[hd] = _dot_nt(q_pos[:, ks], k_neg[:, ks])
        bwd[hd] = _dot_nt(q_neg[:, ks], k_pos[:, ks])
        upd[hd] = [_dot_tn(k_dec[cs, ks], v[cs, vs]) for cs in csl]

    def vector_work(hd):
        scores[hd] = jnp.where(same_chunk, jnp.where(causal, fwd[hd], bwd[hd]), 0.0).astype(BF16)
        st = carry["states"][hd]
        entering[hd] = []
        for n in range(n_chunks):
            entering[hd].append(st.astype(BF16))
            d_n = jnp.concatenate([decay_t[hd][n]] * (head_v // V7X_LANES), axis=-1)
            st = st * d_n + upd[hd][n]
        carry["states"][hd] = st

    def state_matmuls(hd):
        inter = [_dot(q_pos[cs, ksl[hd]], entering[hd][n]) for n, cs in enumerate(csl)]
        o_h = _dot(scores[hd], v[:, vsl[hd]]) + jnp.concatenate(inter, axis=0)
        outs[hd] = o_h * lax.rsqrt(jnp.mean(o_h * o_h, axis=-1, keepdims=True) + RMS_EPS) * hn_ref[...]

    for hd in heads:
        state_free_matmuls(hd)
    yield
    for stage in (vector_work, state_matmuls):
        for hd in heads:
            stage(hd)
            yield
    return (jnp.concatenate([outs[hd] for hd in heads], axis=-1) * gate_act).astype(BF16)


def _sub_block(x_ref, o_ref, rows, first_frame, carry, pool_w, gla_w, fnw_ref, pool_scratch,
               head_k, head_v):
    h = yield from _pool_layer(x_ref[0, rows, :], carry, first_frame, pool_w, *pool_scratch)
    yield
    y2 = yield from _gla_layer(h, carry, gla_w, head_k, head_v)
    h2 = h + _dot(y2, gla_w[-1][...])
    o_ref[0, rows, :] = _rms_norm(h2, fnw_ref[...])


def _load_weights(pairs, stage, sems):
    chunks = []
    for src, dst in pairs:
        n_rows, n_cols = src.shape
        for r0 in range(0, n_rows, WEIGHT_ROWS):
            chunks.append((src, dst, r0, min(WEIGHT_ROWS, n_rows - r0), n_cols))
    ahead = WEIGHT_SLOTS - 1

    def copy(i):
        src, _, r0, n, c = chunks[i]
        slot = i % WEIGHT_SLOTS
        return pltpu.make_async_copy(src.at[r0:r0 + n, :], stage.at[slot, 0:n, 0:c], sems.at[slot])

    for i in range(min(ahead, len(chunks))):
        copy(i).start()
    for i, (_, dst, r0, n, c) in enumerate(chunks):
        if i + ahead < len(chunks):
            copy(i + ahead).start()
        copy(i).wait()
        dst[r0:r0 + n, :] = stage[i % WEIGHT_SLOTS, 0:n, 0:c].astype(BF16)


def _fused_kernel(x_ref, nw_ref, gb_ref, ps_ref, gkw_ref, gkb_ref, hn_ref, fnw_ref,
                  pin_hbm, gw_hbm, pout_hbm, gint_hbm, gout_hbm,
                  o_ref,
                  pin_ref, gw_ref, pout_ref, gint_ref, gout_ref, stage, sems,
                  uext, s1, s2, s3, halo, state,
                  *, ts, d_model, head_k, head_v):
    j = pl.program_id(1)
    n_sub = ts // SUB_ROWS
    lo_row = V7X_SUBLANES

    @pl.when(jnp.logical_and(pl.program_id(0) == 0, j == 0))
    def _():
        _load_weights([(pin_hbm, pin_ref), (gw_hbm, gw_ref), (pout_hbm, pout_ref),
                       (gint_hbm, gint_ref), (gout_hbm, gout_ref)], stage, sems)
        for buf in (uext, s1, s2, s3):
            buf[:, 0:lo_row, :] = jnp.zeros((n_sub, lo_row, buf.shape[2]), F32)

    @pl.when(j == 0)
    def _():
        halo[...] = jnp.zeros_like(halo)
        state[...] = jnp.zeros_like(state)

    pool_w = (nw_ref, pin_ref, gw_ref, gb_ref, ps_ref, pout_ref)
    gla_w = (nw_ref, gint_ref, gkw_ref, gkb_ref, hn_ref, gout_ref)

    carry = {"u_tail": halo[...], "states": [state[hd] for hd in range(GLA_HEADS)]}
    blocks = [
        _sub_block(x_ref, o_ref, slice(s * SUB_ROWS, (s + 1) * SUB_ROWS), j * ts + s * SUB_ROWS,
                   carry, pool_w, gla_w, fnw_ref, (uext.at[s], s1.at[s], s2.at[s], s3.at[s]),
                   head_k, head_v)
        for s in range(n_sub)
    ]
    while blocks:
        for blk in list(blocks):
            try:
                next(blk)
            except StopIteration:
                blocks.remove(blk)
    halo[...] = carry["u_tail"]
    for hd in range(GLA_HEADS):
        state[hd] = carry["states"][hd]


def _resident(shape):
    zeros = (0,) * len(shape)
    return pl.BlockSpec(shape, lambda b, j: zeros, pipeline_mode=pl.Buffered(1))


def kernel(x, norm_w, pool_in_w, pool_group_w, pool_group_b, pool_scale, pool_out_w,
           gla_in_w, gla_gk_w, gla_gk_b, gla_head_norm_w, gla_out_w, final_norm_w):
    batch, seq, d_model = x.shape
    ts = SEQ_TILE
    assert seq % ts == 0 and ts % SUB_ROWS == 0 and SUB_ROWS % CHUNK == 0
    assert norm_w.shape[0] == 2 and pool_in_w.shape[0] == 1 and gla_in_w.shape[0] == 1
    key_w = gla_gk_w.shape[-1]
    head_k = key_w // GLA_HEADS
    head_v = d_model // GLA_HEADS
    n_groups = len(POOL_WINDOWS)
    gdim = d_model // n_groups
    qkvg_w = 2 * key_w + 2 * d_model
    assert gla_in_w.shape[-1] == qkvg_w + GLA_GATE_RANK

    small = (norm_w, pool_group_b[0], pool_scale[0].reshape(1, d_model),
             gla_gk_w[0], gla_gk_b, gla_head_norm_w,
             final_norm_w.reshape(1, d_model))
    big = (pool_in_w[0], pool_group_w[0].reshape(d_model, gdim), pool_out_w[0],
           gla_in_w[0].T, gla_out_w[0])
    assert all(w.shape[1] <= WEIGHT_COLS for w in big)

    operands = (x,) + small + big
    tile = pl.BlockSpec((1, ts, d_model), lambda b, j: (b, j, 0))
    in_specs = ([tile] + [_resident(op.shape) for op in small]
                + [pl.BlockSpec(memory_space=pl.ANY)] * len(big))
    n_sub = ts // SUB_ROWS
    rows = SUB_ROWS + POOL_OFF
    body = functools.partial(_fused_kernel, ts=ts, d_model=d_model, head_k=head_k, head_v=head_v)
    return pl.pallas_call(
        body,
        grid=(batch, seq // ts),
        in_specs=in_specs,
        out_specs=tile,
        out_shape=jax.ShapeDtypeStruct(x.shape, x.dtype),
        scratch_shapes=[
            pltpu.VMEM(pool_in_w.shape[1:], BF16),
            pltpu.VMEM((d_model, gdim), BF16),
            pltpu.VMEM(pool_out_w.shape[1:], BF16),
            pltpu.VMEM((gla_in_w.shape[2], d_model), BF16),
            pltpu.VMEM(gla_out_w.shape[1:], BF16),
            pltpu.VMEM((WEIGHT_SLOTS, WEIGHT_ROWS, WEIGHT_COLS), F32),
            pltpu.SemaphoreType.DMA((WEIGHT_SLOTS,)),
            pltpu.VMEM((n_sub, rows, d_model), F32),
            pltpu.VMEM((n_sub, rows, d_model - gdim), F32),
            pltpu.VMEM((n_sub, rows, d_model - 2 * gdim), F32),
            pltpu.VMEM((n_sub, rows, d_model - 3 * gdim), F32),
            pltpu.VMEM((POOL_HALO, d_model), F32),
            pltpu.VMEM((GLA_HEADS, head_k, head_v), F32),
        ],
        compiler_params=pltpu.CompilerParams(
            dimension_semantics=("arbitrary", "arbitrary"),
            vmem_limit_bytes=V7X_VMEM_LIMIT_BYTES,
        ),
        name="pool_gla_trunk",
    )(*operands)
```

```python
import functools

import jax
import jax.numpy as jnp
from jax import lax
from jax.experimental import pallas as pl
from jax.experimental.pallas import tpu as pltpu

F32 = jnp.float32
BF16 = jnp.bfloat16

RMS_EPS = 1e-6
CHUNK = 64
POOL_WINDOWS = (2, 4, 8, 16)
GLA_HEADS = 4
GLA_GATE_NORMALIZER = 16.0
GLA_GATE_RANK = 16

V7X_LANES = 128
V7X_SUBLANES = 8
V7X_VMEM_LIMIT_BYTES = 56 * 1024 * 1024

SEQ_TILE = 512
SUB_ROWS = 256
WEIGHT_ROWS = 1024
WEIGHT_COLS = 1024
WEIGHT_SLOTS = 3
POOL_HALO = max(POOL_WINDOWS)
POOL_OFF = POOL_HALO + V7X_SUBLANES


def _dot(a, b):
    return jnp.dot(a, b, preferred_element_type=F32)


def _dot_nt(a, b):
    return lax.dot_general(a, b, (((1,), (1,)), ((), ())), preferred_element_type=F32)


def _dot_tn(a, b):
    return lax.dot_general(a, b, (((0,), (0,)), ((), ())), preferred_element_type=F32)


def _rms_norm(x, w):
    return x * lax.rsqrt(jnp.mean(x * x, axis=-1, keepdims=True) + RMS_EPS) * w


def _silu(x):
    return x / (1.0 + jnp.exp(-x))


def _log_sigmoid(z):
    return jnp.minimum(z, 0.0) - jnp.log(1.0 + jnp.exp(-jnp.abs(z)))


def _split_bf16(x):
    hi = x.astype(BF16)
    lo = (x - hi.astype(F32)).astype(BF16)
    return hi, lo


def _pool_layer(x, carry, first_frame, w, uext, s1, s2, s3):
    nw_ref, pin_ref, gw_ref, gb_ref, ps_ref, pout_ref = w
    rows, d_model = x.shape
    gdim = d_model // len(POOL_WINDOWS)
    ext = rows + POOL_HALO
    lo_row = V7X_SUBLANES

    normed = _rms_norm(x, nw_ref[0:1, :]).astype(BF16)
    u = _dot(normed, pin_ref[:, :d_model])
    uext[lo_row:POOL_OFF, :] = carry["u_tail"]
    uext[POOL_OFF:POOL_OFF + rows, :] = u
    carry["u_tail"] = u[rows - POOL_HALO:, :]
    yield
    gate = _dot(normed, pin_ref[:, d_model:])

    a1 = uext[lo_row:lo_row + ext, :] + uext[lo_row - 1:lo_row - 1 + ext, :]
    s1[lo_row:lo_row + ext, :] = a1[:, gdim:]
    a2 = s1[lo_row:lo_row + ext, :] + s1[lo_row - 2:lo_row - 2 + ext, :]
    s2[lo_row:lo_row + ext, :] = a2[:, gdim:]
    a3 = s2[lo_row:lo_row + ext, :] + s2[lo_row - 4:lo_row - 4 + ext, :]
    s3[lo_row:lo_row + ext, :] = a3[:, gdim:]
    a4 = s3[lo_row:lo_row + ext, :] + s3[lo_row - 8:lo_row - 8 + ext, :]
    sums = (a1[POOL_HALO:, :gdim], a2[POOL_HALO:, :gdim], a3[POOL_HALO:, :gdim], a4[POOL_HALO:, :])

    frame = first_frame + lax.broadcasted_iota(jnp.int32, (rows, 1), 0) + 1
    ys = []
    for g, win in enumerate(POOL_WINDOWS):
        cols = slice(g * gdim, (g + 1) * gdim)
        inv_count = 1.0 / jnp.minimum(frame, win).astype(F32)
        pooled = sums[g] * inv_count - u[:, cols]
        mixed = _dot(pooled.astype(BF16), gw_ref[cols, :]) + gb_ref[g:g + 1, :]
        ys.append(mixed * ps_ref[:, cols] * _silu(gate[:, cols]))
        yield
    y = jnp.concatenate(ys, axis=-1).astype(BF16)
    yield
    return x + _dot(y, pout_ref[...])


def _gla_layer(h, carry, w, head_k, head_v):
    nw_ref, gint_ref, gkw_ref, gkb_ref, hn_ref, _ = w
    rows, d_model = h.shape
    n_chunks = rows // CHUNK
    key_w = GLA_HEADS * head_k
    heads = range(GLA_HEADS)
    ksl = [slice(hd * head_k, (hd + 1) * head_k) for hd in heads]
    vsl = [slice(hd * head_v, (hd + 1) * head_v) for hd in heads]
    csl = [slice(n * CHUNK, (n + 1) * CHUNK) for n in range(n_chunks)]

    n2 = _rms_norm(h, nw_ref[1:2, :]).astype(BF16)
    qkvg_w = 2 * (key_w + d_model)
    glow_t = _dot_nt(gint_ref[qkvg_w:, :], n2).astype(BF16)
    qk = _dot_nt(n2, gint_ref[:2 * key_w, :])
    z = _dot_tn(glow_t, gkw_ref[...].astype(BF16)) + gkb_ref[...]
    yield

    vg = _dot_nt(n2, gint_ref[2 * key_w:qkvg_w, :])
    lg = _log_sigmoid(z) * (1.0 / GLA_GATE_NORMALIZER)
    lg_hi, lg_lo = _split_bf16(lg)
    r = lax.broadcasted_iota(jnp.int32, (rows, rows), 0)
    c = lax.broadcasted_iota(jnp.int32, (rows, rows), 1)
    same_chunk = (r // CHUNK) == (c // CHUNK)
    causal = c <= r
    tril = jnp.where(same_chunk & causal, 1.0, 0.0).astype(BF16)
    cum = _dot(tril, lg_hi) + _dot(tril, lg_lo)
    yield

    totals = [cum[cs.stop - 1:cs.stop, :] for cs in csl]
    tot = jnp.concatenate([jnp.broadcast_to(t, (CHUNK, key_w)) for t in totals], axis=0)
    e_pos = jnp.exp(cum)
    e_neg = jnp.exp(-cum)
    q = qk[:, :key_w] * (head_k ** -0.5)
    k = qk[:, key_w:]
    q_pos = (q * e_pos).astype(BF16)
    q_neg = (q * e_neg).astype(BF16)
    k_neg = (k * e_neg).astype(BF16)
    k_pos = (k * e_pos).astype(BF16)
    k_dec = (k * jnp.exp(tot - cum)).astype(BF16)
    v = vg[:, :d_model].astype(BF16)
    gate_act = _silu(vg[:, d_model:])
    decay_t = [[jnp.broadcast_to(jnp.exp(t[:, ks]), (V7X_LANES, head_k)).T for t in totals]
               for ks in ksl]
    yield

    fwd, bwd, upd, scores, entering, outs = {}, {}, {}, {}, {}, {}
    bsl = [slice(b0, b0 + V7X_LANES) for b0 in range(0, rows, V7X_LANES)]
    same_blk = same_chunk[:V7X_LANES, :V7X_LANES]
    causal_blk = causal[:V7X_LANES, :V7X_LANES]

    def state_free_matmuls(hd):
        ks, vs = ksl[hd], vsl[hd]
        fwd[hd] = [_dot_nt(q_pos[bs, ks], k_neg[bs, ks]) for bs in bsl]
        bwd[hd] = [_dot_nt(q_neg[bs, ks], k_pos[bs, ks]) for bs in bsl]
        upd[hd] = [_dot_tn(k_dec[cs, ks], v[cs, vs]) for cs in csl]

    def vector_work(hd):
        scores[hd] = [jnp.where(same_blk, jnp.where(causal_blk, f, b), 0.0).astype(BF16)
                      for f, b in zip(fwd[hd], bwd[hd])]
        st = carry["states"][hd]
        entering[hd] = []
        for n in range(n_chunks):
            entering[hd].append(st.astype(BF16))
            d_n = jnp.concatenate([decay_t[hd][n]] * (head_v // V7X_LANES), axis=-1)
            st = st * d_n + upd[hd][n]
        carry["states"][hd] = st

    def state_matmuls(hd):
        inter = [_dot(q_pos[cs, ksl[hd]], entering[hd][n]) for n, cs in enumerate(csl)]
        intra = [_dot(s_b, v[bs, vsl[hd]]) for s_b, bs in zip(scores[hd], bsl)]
        o_h = jnp.concatenate(intra, axis=0) + jnp.concatenate(inter, axis=0)
        outs[hd] = o_h * lax.rsqrt(jnp.mean(o_h * o_h, axis=-1, keepdims=True) + RMS_EPS) * hn_ref[...]

    for hd in heads:
        state_free_matmuls(hd)
    yield
    for stage in (vector_work, state_matmuls):
        for hd in heads:
            stage(hd)
            yield
    return (jnp.concatenate([outs[hd] for hd in heads], axis=-1) * gate_act).astype(BF16)


def _sub_block(x_ref, o_ref, rows, first_frame, carry, pool_w, gla_w, fnw_ref, pool_scratch,
               head_k, head_v):
    h = yield from _pool_layer(x_ref[0, rows, :], carry, first_frame, pool_w, *pool_scratch)
    yield
    y2 = yield from _gla_layer(h, carry, gla_w, head_k, head_v)
    h2 = h + _dot(y2, gla_w[-1][...])
    o_ref[0, rows, :] = _rms_norm(h2, fnw_ref[...])


def _load_weights(pairs, stage, sems):
    chunks = []
    for src, dst in pairs:
        n_rows, n_cols = src.shape
        for r0 in range(0, n_rows, WEIGHT_ROWS):
            for c0 in range(0, n_cols, WEIGHT_COLS):
                chunks.append((src, dst, r0, min(WEIGHT_ROWS, n_rows - r0),
                               c0, min(WEIGHT_COLS, n_cols - c0)))
    ahead = WEIGHT_SLOTS - 1

    def copy(i):
        src, _, r0, n, c0, c = chunks[i]
        slot = i % WEIGHT_SLOTS
        return pltpu.make_async_copy(src.at[r0:r0 + n, c0:c0 + c], stage.at[slot, 0:n, 0:c],
                                     sems.at[slot])

    for i in range(min(ahead, len(chunks))):
        copy(i).start()
    for i, (_, dst, r0, n, c0, c) in enumerate(chunks):
        if i + ahead < len(chunks):
            copy(i + ahead).start()
        copy(i).wait()
        dst[r0:r0 + n, c0:c0 + c] = stage[i % WEIGHT_SLOTS, 0:n, 0:c].astype(BF16)


def _fused_kernel(x_ref, nw_ref, gb_ref, ps_ref, gkw_ref, gkb_ref, hn_ref, fnw_ref,
                  pin_hbm, gw_hbm, pout_hbm, gint_hbm, gout_hbm,
                  o_ref,
                  pin_ref, gw_ref, pout_ref, gint_ref, gout_ref, stage, sems,
                  uext, s1, s2, s3, halo, state,
                  *, ts, d_model, head_k, head_v):
    j = pl.program_id(1)
    n_sub = ts // SUB_ROWS
    lo_row = V7X_SUBLANES

    @pl.when(jnp.logical_and(pl.program_id(0) == 0, j == 0))
    def _():
        _load_weights([(pin_hbm, pin_ref), (gw_hbm, gw_ref), (pout_hbm, pout_ref),
                       (gint_hbm, gint_ref), (gout_hbm, gout_ref)], stage, sems)
        for buf in (uext, s1, s2, s3):
            buf[:, 0:lo_row, :] = jnp.zeros((n_sub, lo_row, buf.shape[2]), F32)

    @pl.when(j == 0)
    def _():
        halo[...] = jnp.zeros_like(halo)
        state[...] = jnp.zeros_like(state)

    pool_w = (nw_ref, pin_ref, gw_ref, gb_ref, ps_ref, pout_ref)
    gla_w = (nw_ref, gint_ref, gkw_ref, gkb_ref, hn_ref, gout_ref)

    carry = {"u_tail": halo[...], "states": [state[hd] for hd in range(GLA_HEADS)]}
    blocks = [
        _sub_block(x_ref, o_ref, slice(s * SUB_ROWS, (s + 1) * SUB_ROWS), j * ts + s * SUB_ROWS,
                   carry, pool_w, gla_w, fnw_ref, (uext.at[s], s1.at[s], s2.at[s], s3.at[s]),
                   head_k, head_v)
        for s in range(n_sub)
    ]
    while blocks:
        for blk in list(blocks):
            try:
                next(blk)
            except StopIteration:
                blocks.remove(blk)
    halo[...] = carry["u_tail"]
    for hd in range(GLA_HEADS):
        state[hd] = carry["states"][hd]


def _resident(shape):
    zeros = (0,) * len(shape)
    return pl.BlockSpec(shape, lambda b, j: zeros, pipeline_mode=pl.Buffered(1))


def kernel(x, norm_w, pool_in_w, pool_group_w, pool_group_b, pool_scale, pool_out_w,
           gla_in_w, gla_gk_w, gla_gk_b, gla_head_norm_w, gla_out_w, final_norm_w):
    batch, seq, d_model = x.shape
    ts = SEQ_TILE
    assert seq % ts == 0 and ts % SUB_ROWS == 0 and SUB_ROWS % CHUNK == 0
    assert norm_w.shape[0] == 2 and pool_in_w.shape[0] == 1 and gla_in_w.shape[0] == 1
    key_w = gla_gk_w.shape[-1]
    head_k = key_w // GLA_HEADS
    head_v = d_model // GLA_HEADS
    n_groups = len(POOL_WINDOWS)
    gdim = d_model // n_groups
    qkvg_w = 2 * key_w + 2 * d_model
    assert gla_in_w.shape[-1] == qkvg_w + GLA_GATE_RANK

    small = (norm_w, pool_group_b[0], pool_scale[0].reshape(1, d_model),
             gla_gk_w[0], gla_gk_b, gla_head_norm_w,
             final_norm_w.reshape(1, d_model))
    big = (pool_in_w[0], pool_group_w[0].reshape(d_model, gdim), pool_out_w[0],
           gla_in_w[0].T, gla_out_w[0])
    operands = (x,) + small + big
    tile = pl.BlockSpec((1, ts, d_model), lambda b, j: (b, j, 0))
    in_specs = ([tile] + [_resident(op.shape) for op in small]
                + [pl.BlockSpec(memory_space=pl.ANY)] * len(big))
    n_sub = ts // SUB_ROWS
    rows = SUB_ROWS + POOL_OFF
    body = functools.partial(_fused_kernel, ts=ts, d_model=d_model, head_k=head_k, head_v=head_v)
    return pl.pallas_call(
        body,
        grid=(batch, seq // ts),
        in_specs=in_specs,
        out_specs=tile,
        out_shape=jax.ShapeDtypeStruct(x.shape, x.dtype),
        scratch_shapes=[
            pltpu.VMEM(pool_in_w.shape[1:], BF16),
            pltpu.VMEM((d_model, gdim), BF16),
            pltpu.VMEM(pool_out_w.shape[1:], BF16),
            pltpu.VMEM((gla_in_w.shape[2], d_model), BF16),
            pltpu.VMEM(gla_out_w.shape[1:], BF16),
            pltpu.VMEM((WEIGHT_SLOTS, WEIGHT_ROWS, WEIGHT_COLS), F32),
            pltpu.SemaphoreType.DMA((WEIGHT_SLOTS,)),
            pltpu.VMEM((n_sub, rows, d_model), F32),
            pltpu.VMEM((n_sub, rows, d_model - gdim), F32),
            pltpu.VMEM((n_sub, rows, d_model - 2 * gdim), F32),
            pltpu.VMEM((n_sub, rows, d_model - 3 * gdim), F32),
            pltpu.VMEM((POOL_HALO, d_model), F32),
            pltpu.VMEM((GLA_HEADS, head_k, head_v), F32),
        ],
        compiler_params=pltpu.CompilerParams(
            dimension_semantics=("arbitrary", "arbitrary"),
            vmem_limit_bytes=V7X_VMEM_LIMIT_BYTES,
        ),
        name="pool_gla_trunk",
    )(*operands)
```
